```python
import math
import jax, jax.numpy as jnp
from jax import lax
import numpy as np

D_MODEL = 1024
BATCH = 8
SEQ = 4096
DEPTH = 2

N_MIXERS = 2
GRID_W = 64
NA_HEADS = 16
NA_HEAD_DIM = D_MODEL // NA_HEADS
NA_WIN_ROWS = 8
NA_WIN_COLS = 16
FN_GROUPS = 8
FN_GROUP_DIM = D_MODEL // FN_GROUPS
D_FF = ((8 * D_MODEL // 3 + 127) // 128) * 128
CONV_W = 3
LN_EPS = 1e-5
ALPHA = (2.0 * DEPTH) ** 0.25
BETA = (8.0 * DEPTH) ** -0.25
N_NA_LAYERS = (DEPTH + 1) // 2
N_FN_LAYERS = DEPTH // 2

kernel_name = "hybrid_natten_fnet_convffn_deepnorm_adaln"


def layer_norm(x, g, b):
    xf = x.astype(jnp.float32)
    mu = jnp.mean(xf, axis=-1, keepdims=True)
    var = jnp.mean(jnp.square(xf - mu), axis=-1, keepdims=True)
    y = (xf - mu) * lax.rsqrt(var + LN_EPS)
    return (y * g.astype(jnp.float32) + b.astype(jnp.float32)).astype(x.dtype)


def neighborhood_attention(u, w_qkv, rpb, w_o):
    B, S, D = u.shape
    rows = S // GRID_W
    kr = min(NA_WIN_ROWS, rows)
    qkv = jnp.einsum('bsd,de->bse', u, w_qkv)
    q, k, v = jnp.split(qkv, 3, axis=-1)
    grid = lambda t: t.reshape(B, rows, GRID_W, NA_HEADS, NA_HEAD_DIM)
    q = grid(q) * (NA_HEAD_DIM ** -0.5)
    k = grid(k)
    v = grid(v)
    qcol = np.arange(GRID_W)
    col_start = np.clip(qcol - NA_WIN_COLS // 2, 0, GRID_W - NA_WIN_COLS)
    kcol = np.arange(GRID_W)
    col_in = (kcol[None, :] >= col_start[:, None]) & (kcol[None, :] < col_start[:, None] + NA_WIN_COLS)
    dc_idx = np.clip(kcol[None, :] - qcol[:, None] + NA_WIN_COLS - 1, 0, 2 * NA_WIN_COLS - 2)
    row_start = np.clip(np.arange(rows) - kr // 2, 0, rows - kr)
    col_mask = jnp.asarray(col_in)[None, None, :, None, :]

    def one_row(args):
        r, rs = args
        q_r = lax.dynamic_index_in_dim(q, r, axis=1, keepdims=False)
        k_b = lax.dynamic_slice_in_dim(k, rs, kr, axis=1)
        v_b = lax.dynamic_slice_in_dim(v, rs, kr, axis=1)
        s = jnp.einsum('bqhd,bikhd->bhqik', q_r, k_b, preferred_element_type=jnp.float32)
        dr_idx = rs + jnp.arange(kr) - r + NA_WIN_ROWS - 1
        bias = rpb[:, dr_idx[None, :, None], dc_idx[:, None, :]]
        s = s + bias[None].astype(jnp.float32)
        s = jnp.where(col_mask, s, -jnp.inf)
        p = jax.nn.softmax(s.reshape(B, NA_HEADS, GRID_W, kr * GRID_W), axis=-1)
        p = p.reshape(B, NA_HEADS, GRID_W, kr, GRID_W).astype(v_b.dtype)
        return jnp.einsum('bhqik,bikhd->bqhd', p, v_b)

    out = lax.map(one_row, (jnp.arange(rows, dtype=jnp.int32), jnp.asarray(row_start, dtype=jnp.int32)))
    out = jnp.transpose(out, (1, 0, 2, 3, 4)).reshape(B, S, D)
    return jnp.einsum('bsd,de->bse', out, w_o)


def fourier_mix(u, w_o):
    B, S, D = u.shape
    ug = u.astype(jnp.float32).reshape(B, S, FN_GROUPS, FN_GROUP_DIM)
    y = jnp.fft.fftn(ug, axes=(1, 3), norm='ortho').real
    y = y.reshape(B, S, D).astype(u.dtype)
    return jnp.einsum('bsd,de->bse', y, w_o)


def conv_ffn(u, w_up, conv_w, conv_b, w_down):
    S = u.shape[1]
    a, g = jnp.split(jnp.einsum('bsd,df->bsf', u, w_up), 2, axis=-1)
    half = CONV_W // 2
    ap = jnp.pad(a, ((0, 0), (half, half), (0, 0)))
    a = conv_b + sum(ap[:, j:j + S] * conv_w[j] for j in range(CONV_W))
    h = jax.nn.gelu(a) * g
    return jnp.einsum('bsf,fd->bsd', h, w_down)


def setup_inputs(seed: int = 0) -> dict:
    key = jax.random.key(seed)
    ks = jax.random.split(key, 20)
    D, F = D_MODEL, D_FF
    nrm = lambda k, shape, s: jax.random.normal(k, shape, jnp.float32) * s
    x = nrm(ks[0], (BATCH, SEQ, D), 1.0)
    c = nrm(ks[1], (BATCH, D), 1.0)
    ada_w = nrm(ks[2], (DEPTH, D, 6 * D), 0.1 * D ** -0.5)
    ada_b = nrm(ks[3], (DEPTH, 6 * D), 0.01)
    w_qk = nrm(ks[4], (N_NA_LAYERS, D, 2 * D), D ** -0.5)
    w_v = nrm(ks[5], (N_NA_LAYERS, D, D), BETA * D ** -0.5)
    na_w_qkv = jnp.concatenate([w_qk, w_v], axis=-1)
    na_rpb = nrm(ks[6], (N_NA_LAYERS, NA_HEADS, 2 * NA_WIN_ROWS - 1, 2 * NA_WIN_COLS - 1), 0.02)
    na_w_o = nrm(ks[7], (N_NA_LAYERS, D, D), BETA * D ** -0.5)
    fn_w_o = nrm(ks[8], (N_FN_LAYERS, D, D), BETA * D ** -0.5)
    ln1_g = 1.0 + nrm(ks[9], (DEPTH, D), 0.01)
    ln1_b = nrm(ks[10], (DEPTH, D), 0.01)
    ffn_w_up = nrm(ks[11], (DEPTH, D, 2 * F), BETA * D ** -0.5)
    ffn_conv_w = nrm(ks[12], (DEPTH, CONV_W, F), CONV_W ** -0.5)
    ffn_conv_b = nrm(ks[13], (DEPTH, F), 0.01)
    ffn_w_down = nrm(ks[14], (DEPTH, F, D), BETA * F ** -0.5)
    ln2_g = 1.0 + nrm(ks[15], (DEPTH, D), 0.01)
    ln2_b = nrm(ks[16], (DEPTH, D), 0.01)
    return {"x": x, "c": c, "ada_w": ada_w, "ada_b": ada_b,
            "na_w_qkv": na_w_qkv, "na_rpb": na_rpb, "na_w_o": na_w_o, "fn_w_o": fn_w_o,
            "ln1_g": ln1_g, "ln1_b": ln1_b,
            "ffn_w_up": ffn_w_up, "ffn_conv_w": ffn_conv_w, "ffn_conv_b": ffn_conv_b, "ffn_w_down": ffn_w_down,
            "ln2_g": ln2_g, "ln2_b": ln2_b}


def reference(x, c, ada_w, ada_b, na_w_qkv, na_rpb, na_w_o, fn_w_o, ln1_g, ln1_b,
              ffn_w_up, ffn_conv_w, ffn_conv_b, ffn_w_down, ln2_g, ln2_b):
    cs = jax.nn.silu(c)
    for i in range(DEPTH):
        mod = jnp.einsum('bd,de->be', cs, ada_w[i]) + ada_b[i]
        sh1, sc1, g1, sh2, sc2, g2 = jnp.split(mod[:, None, :], 6, axis=-1)
        u = x * (1.0 + sc1) + sh1
        j = i // N_MIXERS
        if i % N_MIXERS == 0:
            y = neighborhood_attention(u, na_w_qkv[j], na_rpb[j], na_w_o[j])
        else:
            y = fourier_mix(u, fn_w_o[j])
        x = layer_norm(ALPHA * x + (1.0 + g1) * y, ln1_g[i], ln1_b[i])
        u = x * (1.0 + sc2) + sh2
        y = conv_ffn(u, ffn_w_up[i], ffn_conv_w[i], ffn_conv_b[i], ffn_w_down[i])
        x = layer_norm(ALPHA * x + (1.0 + g2) * y, ln2_g[i], ln2_b[i])
    return x
```

```python
import functools

import numpy as np
import jax
import jax.numpy as jnp
from jax import lax
from jax.experimental import pallas as pl
from jax.experimental.pallas import tpu as pltpu

F32 = jnp.float32
BF16 = jnp.bfloat16

D_MODEL = 1024
DEPTH = 2
GRID_W = 64
NA_HEADS = 16
NA_HEAD_DIM = D_MODEL // NA_HEADS
NA_WIN_ROWS = 8
NA_WIN_COLS = 16
FN_GROUP_DIM = 128
D_FF = 2816
CONV_W = 3
LN_EPS = 1e-5
ALPHA = (2.0 * DEPTH) ** 0.25

LANES = 128
MXU_DIM = 256
VMEM_LIMIT = 56 * 1024 * 1024

HEAD_GROUP = MXU_DIM // NA_HEAD_DIM
N_HEAD_GROUPS = NA_HEADS // HEAD_GROUP
BAND = NA_WIN_ROWS * GRID_W
FF_CHUNK = MXU_DIM
N_FF_CHUNKS = D_FF // FF_CHUNK
HALO = 16
MASK_VALUE = -1e30


def _params(*sem):
    return pltpu.CompilerParams(dimension_semantics=sem, vmem_limit_bytes=VMEM_LIMIT)


def _layer_norm(h, g, b):
    mu = jnp.mean(h, axis=-1, keepdims=True)
    d = h - mu
    var = jnp.mean(d * d, axis=-1, keepdims=True)
    return d * lax.rsqrt(var + LN_EPS) * g + b


def _mod_kernel(c_ref, w_ref, b_ref, o_ref):
    c = c_ref[...]
    cs = c * jax.nn.sigmoid(c)
    o_ref[0] = jnp.dot(cs, w_ref[0], preferred_element_type=F32,
                       precision=lax.Precision.HIGHEST) + b_ref[0]


def _modulation(c, ada_w, ada_b):
    depth, d, n = ada_w.shape
    bsz = c.shape[0]
    tn = 1536
    return pl.pallas_call(
        _mod_kernel,
        grid=(depth, n // tn),
        in_specs=[pl.BlockSpec((bsz, d), lambda i, j: (0, 0)),
                  pl.BlockSpec((1, d, tn), lambda i, j: (i, 0, j)),
                  pl.BlockSpec((1, 1, tn), lambda i, j: (i, 0, j))],
        out_specs=pl.BlockSpec((1, bsz, tn), lambda i, j: (i, 0, j)),
        out_shape=jax.ShapeDtypeStruct((depth, bsz, n), F32),
        compiler_params=_params("parallel", "parallel"),
        name="adaln_mod",
    )(c, ada_w, ada_b.reshape(depth, 1, n))


def _mod_spec(piece):
    return pl.BlockSpec((1, 1, D_MODEL), lambda b, i: (b, 0, piece))


def _qkv_kernel(x_ref, sh_ref, sc_ref, w_ref, o_ref):
    u = (x_ref[0] * (1.0 + sc_ref[0]) + sh_ref[0]).astype(BF16)
    for j in range(3):
        cols = slice(j * D_MODEL, (j + 1) * D_MODEL)
        acc = jnp.dot(u, w_ref[:, cols], preferred_element_type=F32)
        if j == 0:
            acc = acc * (NA_HEAD_DIM ** -0.5)
        o_ref[0, :, cols] = acc.astype(BF16)


def _qkv_proj(x, mod, w_qkv, tm=512):
    bsz, s, d = x.shape
    return pl.pallas_call(
        _qkv_kernel,
        grid=(bsz, s // tm),
        in_specs=[pl.BlockSpec((1, tm, d), lambda b, i: (b, i, 0)),
                  _mod_spec(0), _mod_spec(1),
                  pl.BlockSpec((d, 3 * d), lambda b, i: (0, 0))],
        out_specs=pl.BlockSpec((1, tm, 3 * d), lambda b, i: (b, i, 0)),
        out_shape=jax.ShapeDtypeStruct((bsz, s, 3 * d), BF16),
        compiler_params=_params("parallel", "parallel"),
        name="qkv_proj",
    )(x, mod, mod, w_qkv)


def _bias_kernel(rpb_ref, o_ref):
    h = pl.program_id(0)
    n_dr = 2 * NA_WIN_ROWS - 1
    n_dc = 2 * NA_WIN_COLS - 1
    q = lax.broadcasted_iota(jnp.int32, (GRID_W, GRID_W), 0)
    k = lax.broadcasted_iota(jnp.int32, (GRID_W, GRID_W), 1)
    col_start = jnp.clip(q - NA_WIN_COLS // 2, 0, GRID_W - NA_WIN_COLS)
    col_in = (k >= col_start) & (k < col_start + NA_WIN_COLS)
    dc = jnp.clip(k - q + NA_WIN_COLS - 1, 0, n_dc - 1)
    for dr in range(n_dr):
        t = jnp.full((GRID_W, GRID_W), MASK_VALUE, F32)
        for d in range(n_dc):
            val = rpb_ref[(h * n_dr + dr) * n_dc + d]
            t = jnp.where(col_in & (dc == d), val, t)
        o_ref[0, dr] = t


def _bias_table(rpb):
    n_dr = 2 * NA_WIN_ROWS - 1
    table = pl.pallas_call(
        _bias_kernel,
        grid=(NA_HEADS,),
        in_specs=[pl.BlockSpec(memory_space=pltpu.SMEM)],
        out_specs=pl.BlockSpec((1, n_dr, GRID_W, GRID_W), lambda h: (h, 0, 0, 0)),
        out_shape=jax.ShapeDtypeStruct((NA_HEADS, n_dr, GRID_W, GRID_W), F32),
        compiler_params=_params("parallel"),
        name="rpb_table",
    )(rpb.reshape(-1))
    bands = jnp.stack([table[:, o:o + NA_WIN_ROWS] for o in range(NA_WIN_ROWS)], axis=0)
    bands = jnp.transpose(bands, (0, 1, 3, 2, 4))
    return bands.reshape(NA_WIN_ROWS, NA_HEADS * GRID_W, BAND)


def _attn_kernel(q_ref, k_ref, v_ref, bias_ref, o_ref, *, rows):
    lane_head = lax.broadcasted_iota(jnp.int32, (GRID_W, MXU_DIM), 1) // NA_HEAD_DIM

    def one_row(r, carry):
        rs = jnp.clip(r - NA_WIN_ROWS // 2, 0, rows - NA_WIN_ROWS)
        off = rs - r + NA_WIN_ROWS - 1
        qr = q_ref[0, pl.ds(pl.multiple_of(r * GRID_W, GRID_W), GRID_W), :]
        zero = jnp.zeros_like(qr)
        q_heads = jnp.concatenate([jnp.where(lane_head == h, qr, zero) for h in range(HEAD_GROUP)], axis=0)
        band = pl.ds(pl.multiple_of(rs * GRID_W, GRID_W), BAND)
        kb = k_ref[0, band, :]
        s = lax.dot_general(q_heads, kb, (((1,), (1,)), ((), ())), preferred_element_type=F32)
        s = s + bias_ref[off]
        m = jnp.max(s, axis=-1, keepdims=True)
        p = jnp.exp(s - m)
        l = jnp.sum(p, axis=-1, keepdims=True)
        vb = v_ref[0, band, :]
        g = jnp.dot(p.astype(BF16), vb, preferred_element_type=F32) / l
        out = jnp.zeros((GRID_W, MXU_DIM), F32)
        for h in range(HEAD_GROUP):
            out = jnp.where(lane_head == h, g[h * GRID_W:(h + 1) * GRID_W, :], out)
        o_ref[0, pl.ds(pl.multiple_of(r * GRID_W, GRID_W), GRID_W), :] = out.astype(BF16)
        return carry

    lax.fori_loop(0, rows, one_row, 0)


def _attention(qkv, bias):
    bsz, s, _ = qkv.shape
    rows = s // GRID_W
    assert rows >= NA_WIN_ROWS
    blk = lambda part: pl.BlockSpec((1, s, MXU_DIM), lambda b, g: (b, 0, part * N_HEAD_GROUPS + g))
    return pl.pallas_call(
        functools.partial(_attn_kernel, rows=rows),
        grid=(bsz, N_HEAD_GROUPS),
        in_specs=[blk(0), blk(1), blk(2),
                  pl.BlockSpec((NA_WIN_ROWS, HEAD_GROUP * GRID_W, BAND), lambda b, g: (0, g, 0))],
        out_specs=pl.BlockSpec((1, s, MXU_DIM), lambda b, g: (b, 0, g)),
        out_shape=jax.ShapeDtypeStruct((bsz, s, D_MODEL), BF16),
        compiler_params=_params("parallel", "parallel"),
        name="nbr_attention",
    )(qkv, qkv, qkv, bias)


def _proj_ln_kernel(y_ref, w_ref, x_ref, gate_ref, g_ref, b_ref, o_ref):
    y = jnp.dot(y_ref[0], w_ref[...], preferred_element_type=F32)
    h = ALPHA * x_ref[0] + (1.0 + gate_ref[0]) * y
    o_ref[0] = _layer_norm(h, g_ref[...], b_ref[...])


def _proj_ln(y, w, x, mod, ln_g, ln_b, tm=512):
    bsz, s, d = x.shape
    vec = pl.BlockSpec((1, d), lambda b, i: (0, 0))
    return pl.pallas_call(
        _proj_ln_kernel,
        grid=(bsz, s // tm),
        in_specs=[pl.BlockSpec((1, tm, d), lambda b, i: (b, i, 0)),
                  pl.BlockSpec((d, d), lambda b, i: (0, 0)),
                  pl.BlockSpec((1, tm, d), lambda b, i: (b, i, 0)),
                  _mod_spec(2), vec, vec],
        out_specs=pl.BlockSpec((1, tm, d), lambda b, i: (b, i, 0)),
        out_shape=jax.ShapeDtypeStruct((bsz, s, d), F32),
        compiler_params=_params("parallel", "parallel"),
        name="proj_ln",
    )(y, w, x, mod, ln_g.reshape(1, d), ln_b.reshape(1, d))


def _gelu_tanh(x):
    c = np.float32(np.sqrt(2.0 / np.pi))
    return x * (0.5 * (1.0 + jnp.tanh(c * (x + 0.044715 * (x * x * x)))))


def _ffn_kernel(x_ref, xp_ref, xn_ref, sh_ref, sc_ref, gate_ref, wa_ref, wg_ref, cw_ref, cb_ref, wd_ref,
                g_ref, b_ref, o_ref, u_ref, acc_ref, *, tm):
    i = pl.program_id(1)
    scale = 1.0 + sc_ref[0]
    shift = sh_ref[0]
    x = x_ref[0]
    prev = jnp.where(i > 0, xp_ref[0] * scale + shift, 0.0)
    nxt = jnp.where(i < pl.num_programs(1) - 1, xn_ref[0] * scale + shift, 0.0)
    u_ref[0:HALO, :] = prev.astype(BF16)
    u_ref[HALO:HALO + tm, :] = (x * scale + shift).astype(BF16)
    u_ref[HALO + tm:, :] = nxt.astype(BF16)
    acc_ref[...] = jnp.zeros_like(acc_ref)
    ext = tm + 2 * HALO

    def chunk(c, carry):
        a = jnp.dot(u_ref[...], wa_ref[c], preferred_element_type=F32)
        gt = jnp.dot(u_ref[HALO:HALO + tm, :], wg_ref[c], preferred_element_type=F32)
        cw = cw_ref[c]
        a_prev = pltpu.roll(a, 1, 0)[HALO:HALO + tm]
        a_next = pltpu.roll(a, ext - 1, 0)[HALO:HALO + tm]
        conv = cb_ref[c] + a_prev * cw[0:1] + a[HALO:HALO + tm] * cw[1:2] + a_next * cw[2:3]
        hid = (_gelu_tanh(conv) * gt).astype(BF16)
        acc_ref[...] += jnp.dot(hid, wd_ref[c], preferred_element_type=F32)
        return carry

    lax.fori_loop(0, N_FF_CHUNKS, chunk, 0)
    h = ALPHA * x + (1.0 + gate_ref[0]) * acc_ref[...]
    o_ref[0] = _layer_norm(h, g_ref[...], b_ref[...])


def _conv_ffn(x, mod, w_up, conv_w, conv_b, w_down, ln_g, ln_b, tm=512):
    bsz, s, d = x.shape
    nblk = s // tm
    hb = tm // HALO
    n_halo = s // HALO
    chunked = lambda w: jnp.transpose(w.reshape(d, N_FF_CHUNKS, FF_CHUNK), (1, 0, 2))
    wa = chunked(w_up[:, :D_FF]).astype(BF16)
    wg = chunked(w_up[:, D_FF:]).astype(BF16)
    wd = w_down.reshape(N_FF_CHUNKS, FF_CHUNK, d).astype(BF16)
    cw = jnp.transpose(conv_w.reshape(CONV_W, N_FF_CHUNKS, FF_CHUNK), (1, 0, 2))
    cb = conv_b.reshape(N_FF_CHUNKS, 1, FF_CHUNK)
    vec = pl.BlockSpec((1, d), lambda b, i: (0, 0))
    once = dict(pipeline_mode=pl.Buffered(1))
    full = lambda a: pl.BlockSpec(a.shape, lambda b, i: (0, 0, 0), **once)
    return pl.pallas_call(
        functools.partial(_ffn_kernel, tm=tm),
        grid=(bsz, nblk),
        in_specs=[pl.BlockSpec((1, tm, d), lambda b, i: (b, i, 0)),
                  pl.BlockSpec((1, HALO, d), lambda b, i: (b, jnp.maximum(i * hb - 1, 0), 0)),
                  pl.BlockSpec((1, HALO, d), lambda b, i: (b, jnp.minimum((i + 1) * hb, n_halo - 1), 0)),
                  _mod_spec(3), _mod_spec(4), _mod_spec(5),
                  full(wa), full(wg), full(cw), full(cb), full(wd), vec, vec],
        out_specs=pl.BlockSpec((1, tm, d), lambda b, i: (b, i, 0)),
        out_shape=jax.ShapeDtypeStruct((bsz, s, d), F32),
        scratch_shapes=[pltpu.VMEM((tm + 2 * HALO, d), BF16), pltpu.VMEM((tm, d), F32)],
        compiler_params=_params("parallel", "arbitrary"),
        name="conv_ffn",
    )(x, x, x, mod, mod, mod, wa, wg, cw, cb, wd, ln_g.reshape(1, d), ln_b.reshape(1, d))


FFT_N1 = 64
FFT_N2 = 64
FFT_ROWS = 512


def _fourier_consts():
    s = FFT_N1 * FFT_N2
    c = np.arange(FN_GROUP_DIM)
    ang = 2.0 * np.pi * np.outer(c, c) / FN_GROUP_DIM
    blk = lambda m: np.kron(np.eye(MXU_DIM // FN_GROUP_DIM), m)
    scale = 1.0 / np.sqrt(s * FN_GROUP_DIM)
    chan = np.concatenate([blk(np.cos(ang)), -blk(np.sin(ang))], axis=1) * scale
    k1 = np.arange(FFT_N1)[None, :, None]
    a = np.arange(FFT_N1)[None, None, :]
    b = np.arange(FFT_N2)[:, None, None]
    th = 2.0 * np.pi * ((k1 * (FFT_N2 * a + b)) % s) / s
    st1 = np.concatenate([np.concatenate([np.cos(th), np.sin(th)], axis=2),
                          np.concatenate([-np.sin(th), np.cos(th)], axis=2)], axis=1)
    k2 = np.arange(FFT_N2)[:, None]
    bb = np.arange(FFT_N2)[None, :]
    ph = 2.0 * np.pi * ((k2 * bb) % FFT_N2) / FFT_N2
    st2 = np.concatenate([np.cos(ph), np.sin(ph)], axis=1)
    f32 = lambda m: jnp.asarray(m.astype(np.float32))
    return f32(chan).astype(BF16), f32(st1).astype(BF16), f32(st2).astype(BF16)


def _fourier_kernel(x_ref, sh_ref, sc_ref, chan_ref, st1_ref, st2_ref, o_ref, z_ref, y_ref):
    s = FFT_N1 * FFT_N2
    n_parts = 2 * MXU_DIM // LANES
    scale = 1.0 + sc_ref[0]
    shift = sh_ref[0]

    def chan_dft(j, carry):
        rows = pl.ds(pl.multiple_of(j * FFT_ROWS, FFT_ROWS), FFT_ROWS)
        u = (x_ref[0, rows, :] * scale + shift).astype(BF16)
        z = jnp.dot(u, chan_ref[...], preferred_element_type=F32)
        for p in range(n_parts):
            z_ref[p, rows, :] = z[:, p * LANES:(p + 1) * LANES]
        return carry

    lax.fori_loop(0, s // FFT_ROWS, chan_dft, 0)

    def gather(ref, take):
        re = jnp.concatenate([ref[0, take, :], ref[1, take, :]], axis=1)
        im = jnp.concatenate([ref[2, take, :], ref[3, take, :]], axis=1)
        return jnp.concatenate([re, im], axis=0).astype(BF16)

    def stage1(b, carry):
        zz = gather(z_ref, pl.ds(b, FFT_N1, stride=FFT_N2))
        y = jnp.dot(st1_ref[b], zz, preferred_element_type=F32)
        put = pl.ds(pl.multiple_of(b * FFT_N1, FFT_N1), FFT_N1)
        for p in range(n_parts):
            half, lane = divmod(p, 2)
            y_ref[p, put, :] = y[half * FFT_N1:(half + 1) * FFT_N1, lane * LANES:(lane + 1) * LANES]
        return carry

    lax.fori_loop(0, FFT_N2, stage1, 0)

    def stage2(k1, carry):
        yy = gather(y_ref, pl.ds(k1, FFT_N2, stride=FFT_N1))
        out = jnp.dot(st2_ref[...], yy, preferred_element_type=F32)
        for lane in range(MXU_DIM // LANES):
            z_ref[lane, pl.ds(k1, FFT_N2, stride=FFT_N1), :] = out[:, lane * LANES:(lane + 1) * LANES]
        return carry

    lax.fori_loop(0, FFT_N1, stage2, 0)
    o_ref[0] = jnp.concatenate([z_ref[0], z_ref[1]], axis=1).astype(BF16)


def _fourier_mix(x, mod):
    bsz, s, d = x.shape
    assert s == FFT_N1 * FFT_N2
    chan, st1, st2 = _fourier_consts()
    mspec = lambda piece: pl.BlockSpec((1, 1, MXU_DIM), lambda b, g: (b, 0, piece * (d // MXU_DIM) + g))
    buf = pltpu.VMEM((2 * MXU_DIM // LANES, s, LANES), F32)
    return pl.pallas_call(
        _fourier_kernel,
        grid=(bsz, d // MXU_DIM),
        in_specs=[pl.BlockSpec((1, s, MXU_DIM), lambda b, g: (b, 0, g)),
                  mspec(0), mspec(1),
                  pl.BlockSpec(chan.shape, lambda b, g: (0, 0)),
                  pl.BlockSpec(st1.shape, lambda b, g: (0, 0, 0)),
                  pl.BlockSpec(st2.shape, lambda b, g: (0, 0))],
        out_specs=pl.BlockSpec((1, s, MXU_DIM), lambda b, g: (b, 0, g)),
        out_shape=jax.ShapeDtypeStruct((bsz, s, d), BF16),
        scratch_shapes=[buf, buf],
        compiler_params=_params("parallel", "parallel"),
        name="fourier_mix",
    )(x, mod, mod, chan, st1, st2)


def kernel(x, c, ada_w, ada_b, na_w_qkv, na_rpb, na_w_o, fn_w_o, ln1_g, ln1_b,
           ffn_w_up, ffn_conv_w, ffn_conv_b, ffn_w_down, ln2_g, ln2_b):
    bsz = x.shape[0]
    mod_all = _modulation(c, ada_w, ada_b)
    for i in range(DEPTH):
        mod = mod_all[i].reshape(bsz, 1, 6 * D_MODEL)
        j = i // 2
        if i % 2 == 0:
            qkv = _qkv_proj(x, mod, na_w_qkv[j].astype(BF16))
            y = _attention(qkv, _bias_table(na_rpb[j]))
            w_o = na_w_o[j]
        else:
            y = _fourier_mix(x, mod)
            w_o = fn_w_o[j]
        x = _proj_ln(y, w_o.astype(BF16), x, mod, ln1_g[i], ln1_b[i])
        x = _conv_ffn(x, mod, ffn_w_up[i], ffn_conv_w[i], ffn_conv_b[i], ffn_w_down[i], ln2_g[i], ln2_b[i])
    return x
```

```python
import functools

import numpy as np
import jax
import jax.numpy as jnp
from jax import lax
from jax.experimental import pallas as pl
from jax.experimental.pallas import tpu as pltpu

F32 = jnp.float32
BF16 = jnp.bfloat16

D_MODEL = 1024
DEPTH = 2
GRID_W = 64
NA_HEADS = 16
NA_HEAD_DIM = D_MODEL // NA_HEADS
NA_WIN_ROWS = 8
NA_WIN_COLS = 16
FN_GROUP_DIM = 128
D_FF = 2816
CONV_W = 3
LN_EPS = 1e-5
ALPHA = (2.0 * DEPTH) ** 0.25

LANES = 128
MXU_DIM = 256
VMEM_LIMIT = 56 * 1024 * 1024

HEAD_GROUP = MXU_DIM // NA_HEAD_DIM
N_HEAD_GROUPS = NA_HEADS // HEAD_GROUP
BAND = NA_WIN_ROWS * GRID_W
FF_CHUNK = MXU_DIM
N_FF_CHUNKS = D_FF // FF_CHUNK
HALO = 16
MASK_VALUE = -1e30
ATTN_UNROLL = 4


def _params(*sem):
    return pltpu.CompilerParams(dimension_semantics=sem, vmem_limit_bytes=VMEM_LIMIT)


def _layer_norm(h, g, b):
    mu = jnp.mean(h, axis=-1, keepdims=True)
    d = h - mu
    var = jnp.mean(d * d, axis=-1, keepdims=True)
    return d * lax.rsqrt(var + LN_EPS) * g + b


def _mod_kernel(c_ref, w_ref, b_ref, o_ref):
    c = c_ref[...]
    cs = c * jax.nn.sigmoid(c)
    o_ref[0] = jnp.dot(cs, w_ref[0], preferred_element_type=F32,
                       precision=lax.Precision.HIGHEST) + b_ref[0]


def _modulation(c, ada_w, ada_b):
    depth, d, n = ada_w.shape
    bsz = c.shape[0]
    tn = 1536
    return pl.pallas_call(
        _mod_kernel,
        grid=(depth, n // tn),
        in_specs=[pl.BlockSpec((bsz, d), lambda i, j: (0, 0)),
                  pl.BlockSpec((1, d, tn), lambda i, j: (i, 0, j)),
                  pl.BlockSpec((1, 1, tn), lambda i, j: (i, 0, j))],
        out_specs=pl.BlockSpec((1, bsz, tn), lambda i, j: (i, 0, j)),
        out_shape=jax.ShapeDtypeStruct((depth, bsz, n), F32),
        compiler_params=_params("parallel", "parallel"),
        name="adaln_mod",
    )(c, ada_w, ada_b.reshape(depth, 1, n))


def _mod_spec(piece):
    return pl.BlockSpec((1, 1, D_MODEL), lambda b, i: (b, 0, piece))


def _qkv_kernel(x_ref, sh_ref, sc_ref, w_ref, o_ref):
    u = (x_ref[0] * (1.0 + sc_ref[0]) + sh_ref[0]).astype(BF16)
    for j in range(3):
        cols = slice(j * D_MODEL, (j + 1) * D_MODEL)
        acc = jnp.dot(u, w_ref[:, cols], preferred_element_type=F32)
        if j == 0:
            acc = acc * (NA_HEAD_DIM ** -0.5)
        o_ref[0, :, cols] = acc.astype(BF16)


def _qkv_proj(x, mod, w_qkv, tm=512):
    bsz, s, d = x.shape
    return pl.pallas_call(
        _qkv_kernel,
        grid=(bsz, s // tm),
        in_specs=[pl.BlockSpec((1, tm, d), lambda b, i: (b, i, 0)),
                  _mod_spec(0), _mod_spec(1),
                  pl.BlockSpec((d, 3 * d), lambda b, i: (0, 0))],
        out_specs=pl.BlockSpec((1, tm, 3 * d), lambda b, i: (b, i, 0)),
        out_shape=jax.ShapeDtypeStruct((bsz, s, 3 * d), BF16),
        compiler_params=_params("parallel", "parallel"),
        name="qkv_proj",
    )(x, mod, mod, w_qkv)


def _bias_kernel(rpb_ref, o_ref):
    h = pl.program_id(0)
    n_dr = 2 * NA_WIN_ROWS - 1
    n_dc = 2 * NA_WIN_COLS - 1
    q = lax.broadcasted_iota(jnp.int32, (GRID_W, GRID_W), 0)
    k = lax.broadcasted_iota(jnp.int32, (GRID_W, GRID_W), 1)
    col_start = jnp.clip(q - NA_WIN_COLS // 2, 0, GRID_W - NA_WIN_COLS)
    col_in = (k >= col_start) & (k < col_start + NA_WIN_COLS)
    dc = jnp.clip(k - q + NA_WIN_COLS - 1, 0, n_dc - 1)
    for dr in range(n_dr):
        t = jnp.full((GRID_W, GRID_W), MASK_VALUE, F32)
        for d in range(n_dc):
            val = rpb_ref[(h * n_dr + dr) * n_dc + d]
            t = jnp.where(col_in & (dc == d), val, t)
        o_ref[0, dr] = t


def _bias_table(rpb):
    n_dr = 2 * NA_WIN_ROWS - 1
    table = pl.pallas_call(
        _bias_kernel,
        grid=(NA_HEADS,),
        in_specs=[pl.BlockSpec(memory_space=pltpu.SMEM)],
        out_specs=pl.BlockSpec((1, n_dr, GRID_W, GRID_W), lambda h: (h, 0, 0, 0)),
        out_shape=jax.ShapeDtypeStruct((NA_HEADS, n_dr, GRID_W, GRID_W), F32),
        compiler_params=_params("parallel"),
        name="rpb_table",
    )(rpb.reshape(-1))
    bands = jnp.stack([table[:, o:o + NA_WIN_ROWS] for o in range(NA_WIN_ROWS)], axis=0)
    bands = jnp.transpose(bands, (0, 1, 3, 2, 4))
    return bands.reshape(NA_WIN_ROWS, NA_HEADS * GRID_W, BAND)


def _attn_kernel(q_ref, k_ref, v_ref, bias_ref, o_ref, *, rows):
    lane_head = lax.broadcasted_iota(jnp.int32, (GRID_W, MXU_DIM), 1) // NA_HEAD_DIM

    def one_row(r, carry):
        rs = jnp.clip(r - NA_WIN_ROWS // 2, 0, rows - NA_WIN_ROWS)
        off = rs - r + NA_WIN_ROWS - 1
        qr = q_ref[0, pl.ds(pl.multiple_of(r * GRID_W, GRID_W), GRID_W), :]
        zero = jnp.zeros_like(qr)
        q_heads = jnp.concatenate([jnp.where(lane_head == h, qr, zero) for h in range(HEAD_GROUP)], axis=0)
        band = pl.ds(pl.multiple_of(rs * GRID_W, GRID_W), BAND)
        kb = k_ref[0, band, :]
        s = lax.dot_general(q_heads, kb, (((1,), (1,)), ((), ())), preferred_element_type=F32)
        s = s + bias_ref[off]
        m = jnp.max(s, axis=-1, keepdims=True)
        p = jnp.exp(s - m)
        l = jnp.sum(p, axis=-1, keepdims=True)
        vb = v_ref[0, band, :]
        g = jnp.dot(p.astype(BF16), vb, preferred_element_type=F32) / l
        out = jnp.zeros((GRID_W, MXU_DIM), F32)
        for h in range(HEAD_GROUP):
            out = jnp.where(lane_head == h, g[h * GRID_W:(h + 1) * GRID_W, :], out)
        o_ref[0, pl.ds(pl.multiple_of(r * GRID_W, GRID_W), GRID_W), :] = out.astype(BF16)
        return carry

    lax.fori_loop(0, rows, one_row, 0, unroll=ATTN_UNROLL)


def _attention(qkv, bias):
    bsz, s, _ = qkv.shape
    rows = s // GRID_W
    assert rows >= NA_WIN_ROWS
    blk = lambda part: pl.BlockSpec((1, s, MXU_DIM), lambda b, g: (b, 0, part * N_HEAD_GROUPS + g))
    return pl.pallas_call(
        functools.partial(_attn_kernel, rows=rows),
        grid=(bsz, N_HEAD_GROUPS),
        in_specs=[blk(0), blk(1), blk(2),
                  pl.BlockSpec((NA_WIN_ROWS, HEAD_GROUP * GRID_W, BAND), lambda b, g: (0, g, 0))],
        out_specs=pl.BlockSpec((1, s, MXU_DIM), lambda b, g: (b, 0, g)),
        out_shape=jax.ShapeDtypeStruct((bsz, s, D_MODEL), BF16),
        compiler_params=_params("parallel", "parallel"),
        name="nbr_attention",
    )(qkv, qkv, qkv, bias)


def _proj_ln_kernel(y_ref, w_ref, x_ref, gate_ref, g_ref, b_ref, o_ref):
    y = jnp.dot(y_ref[0], w_ref[...], preferred_element_type=F32)
    h = ALPHA * x_ref[0] + (1.0 + gate_ref[0]) * y
    o_ref[0] = _layer_norm(h, g_ref[...], b_ref[...])


def _proj_ln(y, w, x, mod, ln_g, ln_b, tm=512):
    bsz, s, d = x.shape
    vec = pl.BlockSpec((1, d), lambda b, i: (0, 0))
    return pl.pallas_call(
        _proj_ln_kernel,
        grid=(bsz, s // tm),
        in_specs=[pl.BlockSpec((1, tm, d), lambda b, i: (b, i, 0)),
                  pl.BlockSpec((d, d), lambda b, i: (0, 0)),
                  pl.BlockSpec((1, tm, d), lambda b, i: (b, i, 0)),
                  _mod_spec(2), vec, vec],
        out_specs=pl.BlockSpec((1, tm, d), lambda b, i: (b, i, 0)),
        out_shape=jax.ShapeDtypeStruct((bsz, s, d), F32),
        compiler_params=_params("parallel", "parallel"),
        name="proj_ln",
    )(y, w, x, mod, ln_g.reshape(1, d), ln_b.reshape(1, d))


def _gelu_tanh(x):
    c = np.float32(np.sqrt(2.0 / np.pi))
    return x * (0.5 * (1.0 + jnp.tanh(c * (x + 0.044715 * (x * x * x)))))


def _ffn_kernel(x_ref, xp_ref, xn_ref, sh_ref, sc_ref, gate_ref, wa_ref, wg_ref, cw_ref, cb_ref, wd_ref,
                g_ref, b_ref, o_ref, u_ref, acc_ref, *, tm):
    i = pl.program_id(1)
    scale = 1.0 + sc_ref[0]
    shift = sh_ref[0]
    x = x_ref[0]
    prev = jnp.where(i > 0, xp_ref[0] * scale + shift, 0.0)
    nxt = jnp.where(i < pl.num_programs(1) - 1, xn_ref[0] * scale + shift, 0.0)
    u_ref[0:HALO, :] = prev.astype(BF16)
    u_ref[HALO:HALO + tm, :] = (x * scale + shift).astype(BF16)
    u_ref[HALO + tm:, :] = nxt.astype(BF16)
    acc_ref[...] = jnp.zeros_like(acc_ref)
    ext = tm + 2 * HALO

    def chunk(c, carry):
        a = jnp.dot(u_ref[...], wa_ref[c], preferred_element_type=F32)
        gt = jnp.dot(u_ref[HALO:HALO + tm, :], wg_ref[c], preferred_element_type=F32)
        cw = cw_ref[c]
        a_prev = pltpu.roll(a, 1, 0)[HALO:HALO + tm]
        a_next = pltpu.roll(a, ext - 1, 0)[HALO:HALO + tm]
        conv = cb_ref[c] + a_prev * cw[0:1] + a[HALO:HALO + tm] * cw[1:2] + a_next * cw[2:3]
        hid = (_gelu_tanh(conv) * gt).astype(BF16)
        acc_ref[...] += jnp.dot(hid, wd_ref[c], preferred_element_type=F32)
        return carry

    lax.fori_loop(0, N_FF_CHUNKS, chunk, 0, unroll=True)
    h = ALPHA * x + (1.0 + gate_ref[0]) * acc_ref[...]
    o_ref[0] = _layer_norm(h, g_ref[...], b_ref[...])


def _conv_ffn(x, mod, w_up, conv_w, conv_b, w_down, ln_g, ln_b, tm=512):
    bsz, s, d = x.shape
    nblk = s // tm
    hb = tm // HALO
    n_halo = s // HALO
    chunked = lambda w: jnp.transpose(w.reshape(d, N_FF_CHUNKS, FF_CHUNK), (1, 0, 2))
    wa = chunked(w_up[:, :D_FF]).astype(BF16)
    wg = chunked(w_up[:, D_FF:]).astype(BF16)
    wd = w_down.reshape(N_FF_CHUNKS, FF_CHUNK, d).astype(BF16)
    cw = jnp.transpose(conv_w.reshape(CONV_W, N_FF_CHUNKS, FF_CHUNK), (1, 0, 2))
    cb = conv_b.reshape(N_FF_CHUNKS, 1, FF_CHUNK)
    vec = pl.BlockSpec((1, d), lambda b, i: (0, 0))
    once = dict(pipeline_mode=pl.Buffered(1))
    full = lambda a: pl.BlockSpec(a.shape, lambda b, i: (0, 0, 0), **once)
    return pl.pallas_call(
        functools.partial(_ffn_kernel, tm=tm),
        grid=(bsz, nblk),
        in_specs=[pl.BlockSpec((1, tm, d), lambda b, i: (b, i, 0)),
                  pl.BlockSpec((1, HALO, d), lambda b, i: (b, jnp.maximum(i * hb - 1, 0), 0)),
                  pl.BlockSpec((1, HALO, d), lambda b, i: (b, jnp.minimum((i + 1) * hb, n_halo - 1), 0)),
                  _mod_spec(3), _mod_spec(4), _mod_spec(5),
                  full(wa), full(wg), full(cw), full(cb), full(wd), vec, vec],
        out_specs=pl.BlockSpec((1, tm, d), lambda b, i: (b, i, 0)),
        out_shape=jax.ShapeDtypeStruct((bsz, s, d), F32),
        scratch_shapes=[pltpu.VMEM((tm + 2 * HALO, d), BF16), pltpu.VMEM((tm, d), F32)],
        compiler_params=_params("parallel", "arbitrary"),
        name="conv_ffn",
    )(x, x, x, mod, mod, mod, wa, wg, cw, cb, wd, ln_g.reshape(1, d), ln_b.reshape(1, d))


FFT_N1 = 64
FFT_N2 = 64
FFT_ROWS = 512
FFT_UNROLL = 8


def _fourier_consts():
    s = FFT_N1 * FFT_N2
    c = np.arange(FN_GROUP_DIM)
    ang = 2.0 * np.pi * np.outer(c, c) / FN_GROUP_DIM
    blk = lambda m: np.kron(np.eye(MXU_DIM // FN_GROUP_DIM), m)
    scale = 1.0 / np.sqrt(s * FN_GROUP_DIM)
    chan = np.concatenate([blk(np.cos(ang)), -blk(np.sin(ang))], axis=1) * scale
    k1 = np.arange(FFT_N1)[None, :, None]
    a = np.arange(FFT_N1)[None, None, :]
    b = np.arange(FFT_N2)[:, None, None]
    th = 2.0 * np.pi * ((k1 * (FFT_N2 * a + b)) % s) / s
    st1 = np.concatenate([np.concatenate([np.cos(th), np.sin(th)], axis=2),
                          np.concatenate([-np.sin(th), np.cos(th)], axis=2)], axis=1)
    k2 = np.arange(FFT_N2)[:, None]
    bb = np.arange(FFT_N2)[None, :]
    ph = 2.0 * np.pi * ((k2 * bb) % FFT_N2) / FFT_N2
    st2 = np.concatenate([np.cos(ph), np.sin(ph)], axis=1)
    f32 = lambda m: jnp.asarray(m.astype(np.float32))
    return f32(chan).astype(BF16), f32(st1).astype(BF16), f32(st2).astype(BF16)


def _fourier_kernel(x_ref, sh_ref, sc_ref, chan_ref, st1_ref, st2_ref, o_ref, z_ref, y_ref):
    s = FFT_N1 * FFT_N2
    n_parts = 2 * MXU_DIM // LANES
    scale = 1.0 + sc_ref[0]
    shift = sh_ref[0]

    def chan_dft(j, carry):
        rows = pl.ds(pl.multiple_of(j * FFT_ROWS, FFT_ROWS), FFT_ROWS)
        u = (x_ref[0, rows, :] * scale + shift).astype(BF16)
        z = jnp.dot(u, chan_ref[...], preferred_element_type=F32)
        for p in range(n_parts):
            z_ref[p, rows, :] = z[:, p * LANES:(p + 1) * LANES]
        return carry

    lax.fori_loop(0, s // FFT_ROWS, chan_dft, 0, unroll=2)

    def gather(ref, take):
        re = jnp.concatenate([ref[0, take, :], ref[1, take, :]], axis=1)
        im = jnp.concatenate([ref[2, take, :], ref[3, take, :]], axis=1)
        return jnp.concatenate([re, im], axis=0).astype(BF16)

    def stage1(b, carry):
        zz = gather(z_ref, pl.ds(b, FFT_N1, stride=FFT_N2))
        y = jnp.dot(st1_ref[b], zz, preferred_element_type=F32)
        put = pl.ds(pl.multiple_of(b * FFT_N1, FFT_N1), FFT_N1)
        for p in range(n_parts):
            half, lane = divmod(p, 2)
            y_ref[p, put, :] = y[half * FFT_N1:(half + 1) * FFT_N1, lane * LANES:(lane + 1) * LANES]
        return carry

    lax.fori_loop(0, FFT_N2, stage1, 0, unroll=FFT_UNROLL)

    def stage2(k1, carry):
        yy = gather(y_ref, pl.ds(k1, FFT_N2, stride=FFT_N1))
        out = jnp.dot(st2_ref[...], yy, preferred_element_type=F32)
        for lane in range(MXU_DIM // LANES):
            z_ref[lane, pl.ds(k1, FFT_N2, stride=FFT_N1), :] = out[:, lane * LANES:(lane + 1) * LANES]
        return carry

    lax.fori_loop(0, FFT_N1, stage2, 0, unroll=FFT_UNROLL)
    o_ref[0] = jnp.concatenate([z_ref[0], z_ref[1]], axis=1).astype(BF16)


def _fourier_mix(x, mod):
    bsz, s, d = x.shape
    assert s == FFT_N1 * FFT_N2
    chan, st1, st2 = _fourier_consts()
    mspec = lambda piece: pl.BlockSpec((1, 1, MXU_DIM), lambda b, g: (b, 0, piece * (d // MXU_DIM) + g))
    buf = pltpu.VMEM((2 * MXU_DIM // LANES, s, LANES), F32)
    return pl.pallas_call(
        _fourier_kernel,
        grid=(bsz, d // MXU_DIM),
        in_specs=[pl.BlockSpec((1, s, MXU_DIM), lambda b, g: (b, 0, g)),
                  mspec(0), mspec(1),
                  pl.BlockSpec(chan.shape, lambda b, g: (0, 0)),
                  pl.BlockSpec(st1.shape, lambda b, g: (0, 0, 0)),
                  pl.BlockSpec(st2.shape, lambda b, g: (0, 0))],
        out_specs=pl.BlockSpec((1, s, MXU_DIM), lambda b, g: (b, 0, g)),
        out_shape=jax.ShapeDtypeStruct((bsz, s, d), BF16),
        scratch_shapes=[buf, buf],
        compiler_params=_params("parallel", "parallel"),
        name="fourier_mix",
    )(x, mod, mod, chan, st1, st2)


def kernel(x, c, ada_w, ada_b, na_w_qkv, na_rpb, na_w_o, fn_w_o, ln1_g, ln1_b,
           ffn_w_up, ffn_conv_w, ffn_conv_b, ffn_w_down, ln2_g, ln2_b):
    bsz = x.shape[0]
    mod_all = _modulation(c, ada_w, ada_b)
    for i in range(DEPTH):
        mod = mod_all[i].reshape(bsz, 1, 6 * D_MODEL)
        j = i // 2
        if i % 2 == 0:
            qkv = _qkv_proj(x, mod, na_w_qkv[j].astype(BF16))
            y = _attention(qkv, _bias_table(na_rpb[j]))
            w_o = na_w_o[j]
        else:
            y = _fourier_mix(x, mod)
            w_o = fn_w_o[j]
        x = _proj_ln(y, w_o.astype(BF16), x, mod, ln1_g[i], ln1_b[i])
        x = _conv_ffn(x, mod, ffn_w_up[i], ffn_conv_w[i], ffn_conv_b[i], ffn_w_down[i], ln2_g[i], ln2_b[i])
    return x
```

```python
import functools

import numpy as np
import jax
import jax.numpy as jnp
from jax import lax
from jax.experimental import pallas as pl
from jax.experimental.pallas import tpu as pltpu

F32 = jnp.float32
BF16 = jnp.bfloat16

D_MODEL = 1024
DEPTH = 2
GRID_W = 64
NA_HEADS = 16
NA_HEAD_DIM = D_MODEL // NA_HEADS
NA_WIN_ROWS = 8
NA_WIN_COLS = 16
FN_GROUP_DIM = 128
D_FF = 2816
CONV_W = 3
LN_EPS = 1e-5
ALPHA = (2.0 * DEPTH) ** 0.25

LANES = 128
MXU_DIM = 256
VMEM_LIMIT = 56 * 1024 * 1024

HEAD_GROUP = MXU_DIM // NA_HEAD_DIM
N_HEAD_GROUPS = NA_HEADS // HEAD_GROUP
BAND = NA_WIN_ROWS * GRID_W
FF_CHUNK = MXU_DIM
N_FF_CHUNKS = D_FF // FF_CHUNK
HALO = 16
MASK_VALUE = -1e30
ATTN_UNROLL = 8


def _params(*sem):
    return pltpu.CompilerParams(dimension_semantics=sem, vmem_limit_bytes=VMEM_LIMIT)


def _layer_norm(h, g, b):
    mu = jnp.mean(h, axis=-1, keepdims=True)
    d = h - mu
    var = jnp.mean(d * d, axis=-1, keepdims=True)
    return d * lax.rsqrt(var + LN_EPS) * g + b


def _mod_kernel(c_ref, w_ref, b_ref, o_ref):
    c = c_ref[...]
    cs = c * jax.nn.sigmoid(c)
    o_ref[0] = jnp.dot(cs, w_ref[0], preferred_element_type=F32,
                       precision=lax.Precision.HIGHEST) + b_ref[0]


def _modulation(c, ada_w, ada_b):
    depth, d, n = ada_w.shape
    bsz = c.shape[0]
    tn = 1536
    return pl.pallas_call(
        _mod_kernel,
        grid=(depth, n // tn),
        in_specs=[pl.BlockSpec((bsz, d), lambda i, j: (0, 0)),
                  pl.BlockSpec((1, d, tn), lambda i, j: (i, 0, j)),
                  pl.BlockSpec((1, 1, tn), lambda i, j: (i, 0, j))],
        out_specs=pl.BlockSpec((1, bsz, tn), lambda i, j: (i, 0, j)),
        out_shape=jax.ShapeDtypeStruct((depth, bsz, n), F32),
        compiler_params=_params("parallel", "parallel"),
        name="adaln_mod",
    )(c, ada_w, ada_b.reshape(depth, 1, n))


def _mod_spec(piece):
    return pl.BlockSpec((1, 1, D_MODEL), lambda b, i: (b, 0, piece))


def _qkv_kernel(x_ref, sh_ref, sc_ref, w_ref, o_ref):
    u = (x_ref[0] * (1.0 + sc_ref[0]) + sh_ref[0]).astype(BF16)
    for j in range(3):
        cols = slice(j * D_MODEL, (j + 1) * D_MODEL)
        acc = jnp.dot(u, w_ref[:, cols], preferred_element_type=F32)
        if j == 0:
            acc = acc * (NA_HEAD_DIM ** -0.5)
        o_ref[0, :, cols] = acc.astype(BF16)


def _qkv_proj(x, mod, w_qkv, tm=512):
    bsz, s, d = x.shape
    return pl.pallas_call(
        _qkv_kernel,
        grid=(bsz, s // tm),
        in_specs=[pl.BlockSpec((1, tm, d), lambda b, i: (b, i, 0)),
                  _mod_spec(0), _mod_spec(1),
                  pl.BlockSpec((d, 3 * d), lambda b, i: (0, 0))],
        out_specs=pl.BlockSpec((1, tm, 3 * d), lambda b, i: (b, i, 0)),
        out_shape=jax.ShapeDtypeStruct((bsz, s, 3 * d), BF16),
        compiler_params=_params("parallel", "parallel"),
        name="qkv_proj",
    )(x, mod, mod, w_qkv)


def _bias_kernel(rpb_ref, o_ref):
    h = pl.program_id(0)
    n_dr = 2 * NA_WIN_ROWS - 1
    n_dc = 2 * NA_WIN_COLS - 1
    q = lax.broadcasted_iota(jnp.int32, (GRID_W, GRID_W), 0)
    k = lax.broadcasted_iota(jnp.int32, (GRID_W, GRID_W), 1)
    col_start = jnp.clip(q - NA_WIN_COLS // 2, 0, GRID_W - NA_WIN_COLS)
    col_in = (k >= col_start) & (k < col_start + NA_WIN_COLS)
    dc = jnp.clip(k - q + NA_WIN_COLS - 1, 0, n_dc - 1)
    for dr in range(n_dr):
        t = jnp.full((GRID_W, GRID_W), MASK_VALUE, F32)
        for d in range(n_dc):
            val = rpb_ref[(h * n_dr + dr) * n_dc + d]
            t = jnp.where(col_in & (dc == d), val, t)
        o_ref[0, dr] = t


def _bias_table(rpb):
    n_dr = 2 * NA_WIN_ROWS - 1
    table = pl.pallas_call(
        _bias_kernel,
        grid=(NA_HEADS,),
        in_specs=[pl.BlockSpec(memory_space=pltpu.SMEM)],
        out_specs=pl.BlockSpec((1, n_dr, GRID_W, GRID_W), lambda h: (h, 0, 0, 0)),
        out_shape=jax.ShapeDtypeStruct((NA_HEADS, n_dr, GRID_W, GRID_W), F32),
        compiler_params=_params("parallel"),
        name="rpb_table",
    )(rpb.reshape(-1))
    bands = jnp.stack([table[:, o:o + NA_WIN_ROWS] for o in range(NA_WIN_ROWS)], axis=0)
    bands = jnp.transpose(bands, (0, 1, 3, 2, 4))
    return bands.reshape(NA_WIN_ROWS, NA_HEADS * GRID_W, BAND)


def _attn_kernel(q_ref, k_ref, v_ref, bias_ref, o_ref, *, rows):
    lane_head = lax.broadcasted_iota(jnp.int32, (GRID_W, MXU_DIM), 1) // NA_HEAD_DIM

    def one_row(r, carry):
        rs = jnp.clip(r - NA_WIN_ROWS // 2, 0, rows - NA_WIN_ROWS)
        off = rs - r + NA_WIN_ROWS - 1
        qr = q_ref[0, pl.ds(pl.multiple_of(r * GRID_W, GRID_W), GRID_W), :]
        zero = jnp.zeros_like(qr)
        q_heads = jnp.concatenate([jnp.where(lane_head == h, qr, zero) for h in range(HEAD_GROUP)], axis=0)
        band = pl.ds(pl.multiple_of(rs * GRID_W, GRID_W), BAND)
        kb = k_ref[0, band, :]
        s = lax.dot_general(q_heads, kb, (((1,), (1,)), ((), ())), preferred_element_type=F32)
        s = s + bias_ref[off]
        m = jnp.max(s, axis=-1, keepdims=True)
        p = jnp.exp(s - m)
        l = jnp.sum(p, axis=-1, keepdims=True)
        vb = v_ref[0, band, :]
        g = jnp.dot(p.astype(BF16), vb, preferred_element_type=F32) / l
        out = jnp.zeros((GRID_W, MXU_DIM), F32)
        for h in range(HEAD_GROUP):
            out = jnp.where(lane_head == h, g[h * GRID_W:(h + 1) * GRID_W, :], out)
        o_ref[0, pl.ds(pl.multiple_of(r * GRID_W, GRID_W), GRID_W), :] = out.astype(BF16)
        return carry

    lax.fori_loop(0, rows, one_row, 0, unroll=ATTN_UNROLL)


def _attention(qkv, bias):
    bsz, s, _ = qkv.shape
    rows = s // GRID_W
    assert rows >= NA_WIN_ROWS
    blk = lambda part: pl.BlockSpec((1, s, MXU_DIM), lambda b, g: (b, 0, part * N_HEAD_GROUPS + g))
    return pl.pallas_call(
        functools.partial(_attn_kernel, rows=rows),
        grid=(bsz, N_HEAD_GROUPS),
        in_specs=[blk(0), blk(1), blk(2),
                  pl.BlockSpec((NA_WIN_ROWS, HEAD_GROUP * GRID_W, BAND), lambda b, g: (0, g, 0))],
        out_specs=pl.BlockSpec((1, s, MXU_DIM), lambda b, g: (b, 0, g)),
        out_shape=jax.ShapeDtypeStruct((bsz, s, D_MODEL), BF16),
        compiler_params=_params("parallel", "parallel"),
        name="nbr_attention",
    )(qkv, qkv, qkv, bias)


def _proj_ln_kernel(y_ref, w_ref, x_ref, gate_ref, g_ref, b_ref, o_ref):
    y = jnp.dot(y_ref[0], w_ref[...], preferred_element_type=F32)
    h = ALPHA * x_ref[0] + (1.0 + gate_ref[0]) * y
    o_ref[0] = _layer_norm(h, g_ref[...], b_ref[...])


def _proj_ln(y, w, x, mod, ln_g, ln_b, tm=512):
    bsz, s, d = x.shape
    vec = pl.BlockSpec((1, d), lambda b, i: (0, 0))
    return pl.pallas_call(
        _proj_ln_kernel,
        grid=(bsz, s // tm),
        in_specs=[pl.BlockSpec((1, tm, d), lambda b, i: (b, i, 0)),
                  pl.BlockSpec((d, d), lambda b, i: (0, 0)),
                  pl.BlockSpec((1, tm, d), lambda b, i: (b, i, 0)),
                  _mod_spec(2), vec, vec],
        out_specs=pl.BlockSpec((1, tm, d), lambda b, i: (b, i, 0)),
        out_shape=jax.ShapeDtypeStruct((bsz, s, d), F32),
        compiler_params=_params("parallel", "parallel"),
        name="proj_ln",
    )(y, w, x, mod, ln_g.reshape(1, d), ln_b.reshape(1, d))


def _gelu_tanh(x):
    c = np.float32(np.sqrt(2.0 / np.pi))
    return x * (0.5 * (1.0 + jnp.tanh(c * (x + 0.044715 * (x * x * x)))))


def _ffn_kernel(x_ref, xp_ref, xn_ref, sh_ref, sc_ref, gate_ref, wa_ref, wg_ref, cw_ref, cb_ref, wd_ref,
                g_ref, b_ref, o_ref, u_ref, acc_ref, *, tm):
    i = pl.program_id(1)
    scale = 1.0 + sc_ref[0]
    shift = sh_ref[0]
    x = x_ref[0]
    prev = jnp.where(i > 0, xp_ref[0] * scale + shift, 0.0)
    nxt = jnp.where(i < pl.num_programs(1) - 1, xn_ref[0] * scale + shift, 0.0)
    u_ref[0:HALO, :] = prev.astype(BF16)
    u_ref[HALO:HALO + tm, :] = (x * scale + shift).astype(BF16)
    u_ref[HALO + tm:, :] = nxt.astype(BF16)
    acc_ref[...] = jnp.zeros_like(acc_ref)
    ext = tm + 2 * HALO

    def up_proj(c):
        a = jnp.dot(u_ref[...], wa_ref[c], preferred_element_type=F32)
        gt = jnp.dot(u_ref[HALO:HALO + tm, :], wg_ref[c], preferred_element_type=F32)
        return a, gt

    a, gt = up_proj(0)
    for c in range(N_FF_CHUNKS):
        ahead = up_proj(c + 1) if c + 1 < N_FF_CHUNKS else None
        cw = cw_ref[c]
        a_prev = pltpu.roll(a, 1, 0)[HALO:HALO + tm]
        a_next = pltpu.roll(a, ext - 1, 0)[HALO:HALO + tm]
        conv = cb_ref[c] + a_prev * cw[0:1] + a[HALO:HALO + tm] * cw[1:2] + a_next * cw[2:3]
        hid = (_gelu_tanh(conv) * gt).astype(BF16)
        acc_ref[...] += jnp.dot(hid, wd_ref[c], preferred_element_type=F32)
        if ahead is not None:
            a, gt = ahead

    h = ALPHA * x + (1.0 + gate_ref[0]) * acc_ref[...]
    o_ref[0] = _layer_norm(h, g_ref[...], b_ref[...])


def _conv_ffn(x, mod, w_up, conv_w, conv_b, w_down, ln_g, ln_b, tm=512):
    bsz, s, d = x.shape
    nblk = s // tm
    hb = tm // HALO
    n_halo = s // HALO
    chunked = lambda w: jnp.transpose(w.reshape(d, N_FF_CHUNKS, FF_CHUNK), (1, 0, 2))
    wa = chunked(w_up[:, :D_FF]).astype(BF16)
    wg = chunked(w_up[:, D_FF:]).astype(BF16)
    wd = w_down.reshape(N_FF_CHUNKS, FF_CHUNK, d).astype(BF16)
    cw = jnp.transpose(conv_w.reshape(CONV_W, N_FF_CHUNKS, FF_CHUNK), (1, 0, 2))
    cb = conv_b.reshape(N_FF_CHUNKS, 1, FF_CHUNK)
    vec = pl.BlockSpec((1, d), lambda b, i: (0, 0))
    once = dict(pipeline_mode=pl.Buffered(1))
    full = lambda a: pl.BlockSpec(a.shape, lambda b, i: (0, 0, 0), **once)
    return pl.pallas_call(
        functools.partial(_ffn_kernel, tm=tm),
        grid=(bsz, nblk),
        in_specs=[pl.BlockSpec((1, tm, d), lambda b, i: (b, i, 0)),
                  pl.BlockSpec((1, HALO, d), lambda b, i: (b, jnp.maximum(i * hb - 1, 0), 0)),
                  pl.BlockSpec((1, HALO, d), lambda b, i: (b, jnp.minimum((i + 1) * hb, n_halo - 1), 0)),
                  _mod_spec(3), _mod_spec(4), _mod_spec(5),
                  full(wa), full(wg), full(cw), full(cb), full(wd), vec, vec],
        out_specs=pl.BlockSpec((1, tm, d), lambda b, i: (b, i, 0)),
        out_shape=jax.ShapeDtypeStruct((bsz, s, d), F32),
        scratch_shapes=[pltpu.VMEM((tm + 2 * HALO, d), BF16), pltpu.VMEM((tm, d), F32)],
        compiler_params=_params("parallel", "arbitrary"),
        name="conv_ffn",
    )(x, x, x, mod, mod, mod, wa, wg, cw, cb, wd, ln_g.reshape(1, d), ln_b.reshape(1, d))


FFT_N1 = 64
FFT_N2 = 64
FFT_ROWS = 512
FFT_UNROLL = 8
FFT_PITCH = 72


def _fourier_consts():
    s = FFT_N1 * FFT_N2
    c = np.arange(FN_GROUP_DIM)
    ang = 2.0 * np.pi * np.outer(c, c) / FN_GROUP_DIM
    blk = lambda m: np.kron(np.eye(MXU_DIM // FN_GROUP_DIM), m)
    scale = 1.0 / np.sqrt(s * FN_GROUP_DIM)
    chan = np.concatenate([blk(np.cos(ang)), -blk(np.sin(ang))], axis=1) * scale
    k1 = np.arange(FFT_N1)[None, :, None]
    a = np.arange(FFT_N1)[None, None, :]
    b = np.arange(FFT_N2)[:, None, None]
    th = 2.0 * np.pi * ((k1 * (FFT_N2 * a + b)) % s) / s
    st1 = np.concatenate([np.concatenate([np.cos(th), np.sin(th)], axis=2),
                          np.concatenate([-np.sin(th), np.cos(th)], axis=2)], axis=1)
    k2 = np.arange(FFT_N2)[:, None]
    bb = np.arange(FFT_N2)[None, :]
    ph = 2.0 * np.pi * ((k2 * bb) % FFT_N2) / FFT_N2
    st2 = np.concatenate([np.cos(ph), np.sin(ph)], axis=1)
    f32 = lambda m: jnp.asarray(m.astype(np.float32))
    return f32(chan).astype(BF16), f32(st1).astype(BF16), f32(st2).astype(BF16)


def _fourier_kernel(x_ref, sh_ref, sc_ref, chan_ref, st1_ref, st2_ref, o_ref, z_ref, y_ref):
    s = FFT_N1 * FFT_N2
    n_parts = 2 * MXU_DIM // LANES
    groups = FFT_ROWS // FFT_N2
    scale = 1.0 + sc_ref[0]
    shift = sh_ref[0]
    group_rows = lambda g: pl.ds(pl.multiple_of(g * FFT_PITCH, 8), FFT_N2)

    def chan_dft(j, carry):
        rows = pl.ds(pl.multiple_of(j * FFT_ROWS, FFT_ROWS), FFT_ROWS)
        u = (x_ref[0, rows, :] * scale + shift).astype(BF16)
        z = jnp.dot(u, chan_ref[...], preferred_element_type=F32)
        for i in range(groups):
            for p in range(n_parts):
                z_ref[p, group_rows(j * groups + i), :] = z[i * FFT_N2:(i + 1) * FFT_N2, p * LANES:(p + 1) * LANES]
        return carry

    lax.fori_loop(0, s // FFT_ROWS, chan_dft, 0, unroll=2)

    def gather(ref, r):
        take = pl.ds(r, FFT_N1, stride=FFT_PITCH)
        re = jnp.concatenate([ref[0, take, :], ref[1, take, :]], axis=1)
        im = jnp.concatenate([ref[2, take, :], ref[3, take, :]], axis=1)
        return jnp.concatenate([re, im], axis=0).astype(BF16)

    def stage1(b, carry):
        y = jnp.dot(st1_ref[b], gather(z_ref, b), preferred_element_type=F32)
        for p in range(n_parts):
            half, lane = divmod(p, 2)
            y_ref[p, group_rows(b), :] = y[half * FFT_N1:(half + 1) * FFT_N1, lane * LANES:(lane + 1) * LANES]
        return carry

    lax.fori_loop(0, FFT_N2, stage1, 0, unroll=FFT_UNROLL)

    def stage2(k1, carry):
        out = jnp.dot(st2_ref[...], gather(y_ref, k1), preferred_element_type=F32)
        for lane in range(MXU_DIM // LANES):
            z_ref[lane, pl.ds(k1, FFT_N2, stride=FFT_PITCH), :] = out[:, lane * LANES:(lane + 1) * LANES]
        return carry

    lax.fori_loop(0, FFT_N1, stage2, 0, unroll=FFT_UNROLL)
    for k2 in range(FFT_N2):
        src = slice(k2 * FFT_PITCH, k2 * FFT_PITCH + FFT_N1)
        o_ref[0, k2 * FFT_N1:(k2 + 1) * FFT_N1, :] = jnp.concatenate(
            [z_ref[0, src, :], z_ref[1, src, :]], axis=1).astype(BF16)


def _fourier_mix(x, mod):
    bsz, s, d = x.shape
    assert s == FFT_N1 * FFT_N2
    chan, st1, st2 = _fourier_consts()
    mspec = lambda piece: pl.BlockSpec((1, 1, MXU_DIM), lambda b, g: (b, 0, piece * (d // MXU_DIM) + g))
    buf = pltpu.VMEM((2 * MXU_DIM // LANES, FFT_N1 * FFT_PITCH, LANES), F32)
    return pl.pallas_call(
        _fourier_kernel,
        grid=(bsz, d // MXU_DIM),
        in_specs=[pl.BlockSpec((1, s, MXU_DIM), lambda b, g: (b, 0, g)),
                  mspec(0), mspec(1),
                  pl.BlockSpec(chan.shape, lambda b, g: (0, 0)),
                  pl.BlockSpec(st1.shape, lambda b, g: (0, 0, 0)),
                  pl.BlockSpec(st2.shape, lambda b, g: (0, 0))],
        out_specs=pl.BlockSpec((1, s, MXU_DIM), lambda b, g: (b, 0, g)),
        out_shape=jax.ShapeDtypeStruct((bsz, s, d), BF16),
        scratch_shapes=[buf, buf],
        compiler_params=_params("parallel", "parallel"),
        name="fourier_mix",
    )(x, mod, mod, chan, st1, st2)


def kernel(x, c, ada_w, ada_b, na_w_qkv, na_rpb, na_w_o, fn_w_o, ln1_g, ln1_b,
           ffn_w_up, ffn_conv_w, ffn_conv_b, ffn_w_down, ln2_g, ln2_b):
    bsz = x.shape[0]
    mod_all = _modulation(c, ada_w, ada_b)
    for i in range(DEPTH):
        mod = mod_all[i].reshape(bsz, 1, 6 * D_MODEL)
        j = i // 2
        if i % 2 == 0:
            qkv = _qkv_proj(x, mod, na_w_qkv[j].astype(BF16))
            y = _attention(qkv, _bias_table(na_rpb[j]))
            w_o = na_w_o[j]
        else:
            y = _fourier_mix(x, mod)
            w_o = fn_w_o[j]
        x = _proj_ln(y, w_o.astype(BF16), x, mod, ln1_g[i], ln1_b[i])
        x = _conv_ffn(x, mod, ffn_w_up[i], ffn_conv_w[i], ffn_conv_b[i], ffn_w_down[i], ln2_g[i], ln2_b[i])
    return x
```

```python
import functools

import numpy as np
import jax
import jax.numpy as jnp
from jax import lax
from jax.experimental import pallas as pl
from jax.experimental.pallas import tpu as pltpu

F32 = jnp.float32
BF16 = jnp.bfloat16

D_MODEL = 1024
DEPTH = 2
GRID_W = 64
NA_HEADS = 16
NA_HEAD_DIM = D_MODEL // NA_HEADS
NA_WIN_ROWS = 8
NA_WIN_COLS = 16
FN_GROUP_DIM = 128
D_FF = 2816
CONV_W = 3
LN_EPS = 1e-5
ALPHA = (2.0 * DEPTH) ** 0.25

LANES = 128
MXU_DIM = 256
VMEM_LIMIT = 56 * 1024 * 1024

HEAD_GROUP = MXU_DIM // NA_HEAD_DIM
N_HEAD_GROUPS = NA_HEADS // HEAD_GROUP
BAND = NA_WIN_ROWS * GRID_W
FF_CHUNK = MXU_DIM
N_FF_CHUNKS = D_FF // FF_CHUNK
FF_DOWN_GROUP = 6
HALO = 16
MASK_VALUE = -1e30
ATTN_UNROLL = 8


def _params(*sem):
    return pltpu.CompilerParams(dimension_semantics=sem, vmem_limit_bytes=VMEM_LIMIT)


def _layer_norm(h, g, b):
    mu = jnp.mean(h, axis=-1, keepdims=True)
    d = h - mu
    var = jnp.mean(d * d, axis=-1, keepdims=True)
    return d * lax.rsqrt(var + LN_EPS) * g + b


def _mod_kernel(c_ref, w_ref, b_ref, o_ref):
    c = c_ref[...]
    cs = c * jax.nn.sigmoid(c)
    o_ref[0] = jnp.dot(cs, w_ref[0], preferred_element_type=F32,
                       precision=lax.Precision.HIGHEST) + b_ref[0]


def _modulation(c, ada_w, ada_b):
    depth, d, n = ada_w.shape
    bsz = c.shape[0]
    tn = 1536
    return pl.pallas_call(
        _mod_kernel,
        grid=(depth, n // tn),
        in_specs=[pl.BlockSpec((bsz, d), lambda i, j: (0, 0)),
                  pl.BlockSpec((1, d, tn), lambda i, j: (i, 0, j)),
                  pl.BlockSpec((1, 1, tn), lambda i, j: (i, 0, j))],
        out_specs=pl.BlockSpec((1, bsz, tn), lambda i, j: (i, 0, j)),
        out_shape=jax.ShapeDtypeStruct((depth, bsz, n), F32),
        compiler_params=_params("parallel", "parallel"),
        name="adaln_mod",
    )(c, ada_w, ada_b.reshape(depth, 1, n))


def _mod_spec(piece):
    return pl.BlockSpec((1, 1, D_MODEL), lambda b, i: (b, 0, piece))


def _qkv_kernel(x_ref, sh_ref, sc_ref, w_ref, o_ref):
    u = (x_ref[0] * (1.0 + sc_ref[0]) + sh_ref[0]).astype(BF16)
    for j in range(3):
        cols = slice(j * D_MODEL, (j + 1) * D_MODEL)
        acc = jnp.dot(u, w_ref[:, cols], preferred_element_type=F32)
        if j == 0:
            acc = acc * (NA_HEAD_DIM ** -0.5)
        o_ref[0, :, cols] = acc.astype(BF16)


def _qkv_proj(x, mod, w_qkv, tm=512):
    bsz, s, d = x.shape
    return pl.pallas_call(
        _qkv_kernel,
        grid=(bsz, s // tm),
        in_specs=[pl.BlockSpec((1, tm, d), lambda b, i: (b, i, 0)),
                  _mod_spec(0), _mod_spec(1),
                  pl.BlockSpec((d, 3 * d), lambda b, i: (0, 0))],
        out_specs=pl.BlockSpec((1, tm, 3 * d), lambda b, i: (b, i, 0)),
        out_shape=jax.ShapeDtypeStruct((bsz, s, 3 * d), BF16),
        compiler_params=_params("parallel", "parallel"),
        name="qkv_proj",
    )(x, mod, mod, w_qkv)


def _bias_kernel(rpb_ref, o_ref):
    h = pl.program_id(0)
    n_dr = 2 * NA_WIN_ROWS - 1
    n_dc = 2 * NA_WIN_COLS - 1
    q = lax.broadcasted_iota(jnp.int32, (GRID_W, GRID_W), 0)
    k = lax.broadcasted_iota(jnp.int32, (GRID_W, GRID_W), 1)
    col_start = jnp.clip(q - NA_WIN_COLS // 2, 0, GRID_W - NA_WIN_COLS)
    col_in = (k >= col_start) & (k < col_start + NA_WIN_COLS)
    dc = jnp.clip(k - q + NA_WIN_COLS - 1, 0, n_dc - 1)
    for dr in range(n_dr):
        t = jnp.full((GRID_W, GRID_W), MASK_VALUE, F32)
        for d in range(n_dc):
            val = rpb_ref[(h * n_dr + dr) * n_dc + d]
            t = jnp.where(col_in & (dc == d), val, t)
        o_ref[0, dr] = t


def _bias_table(rpb):
    n_dr = 2 * NA_WIN_ROWS - 1
    table = pl.pallas_call(
        _bias_kernel,
        grid=(NA_HEADS,),
        in_specs=[pl.BlockSpec(memory_space=pltpu.SMEM)],
        out_specs=pl.BlockSpec((1, n_dr, GRID_W, GRID_W), lambda h: (h, 0, 0, 0)),
        out_shape=jax.ShapeDtypeStruct((NA_HEADS, n_dr, GRID_W, GRID_W), F32),
        compiler_params=_params("parallel"),
        name="rpb_table",
    )(rpb.reshape(-1))
    bands = jnp.stack([table[:, o:o + NA_WIN_ROWS] for o in range(NA_WIN_ROWS)], axis=0)
    bands = jnp.transpose(bands, (0, 1, 3, 2, 4))
    return bands.reshape(NA_WIN_ROWS, NA_HEADS * GRID_W, BAND)


def _attn_kernel(q_ref, k_ref, v_ref, bias_ref, o_ref, *, rows):
    lane_head = lax.broadcasted_iota(jnp.int32, (GRID_W, MXU_DIM), 1) // NA_HEAD_DIM

    def one_row(r, carry):
        rs = jnp.clip(r - NA_WIN_ROWS // 2, 0, rows - NA_WIN_ROWS)
        off = rs - r + NA_WIN_ROWS - 1
        qr = q_ref[0, pl.ds(pl.multiple_of(r * GRID_W, GRID_W), GRID_W), :]
        zero = jnp.zeros_like(qr)
        q_heads = jnp.concatenate([jnp.where(lane_head == h, qr, zero) for h in range(HEAD_GROUP)], axis=0)
        band = pl.ds(pl.multiple_of(rs * GRID_W, GRID_W), BAND)
        kb = k_ref[0, band, :]
        s = lax.dot_general(q_heads, kb, (((1,), (1,)), ((), ())), preferred_element_type=F32)
        s = s + bias_ref[off]
        m = jnp.max(s, axis=-1, keepdims=True)
        p = jnp.exp(s - m)
        l = jnp.sum(p, axis=-1, keepdims=True)
        vb = v_ref[0, band, :]
        g = jnp.dot(p.astype(BF16), vb, preferred_element_type=F32) / l
        out = jnp.zeros((GRID_W, MXU_DIM), F32)
        for h in range(HEAD_GROUP):
            out = jnp.where(lane_head == h, g[h * GRID_W:(h + 1) * GRID_W, :], out)
        o_ref[0, pl.ds(pl.multiple_of(r * GRID_W, GRID_W), GRID_W), :] = out.astype(BF16)
        return carry

    lax.fori_loop(0, rows, one_row, 0, unroll=ATTN_UNROLL)


def _attention(qkv, bias):
    bsz, s, _ = qkv.shape
    rows = s // GRID_W
    assert rows >= NA_WIN_ROWS
    blk = lambda part: pl.BlockSpec((1, s, MXU_DIM), lambda b, g: (b, 0, part * N_HEAD_GROUPS + g))
    return pl.pallas_call(
        functools.partial(_attn_kernel, rows=rows),
        grid=(bsz, N_HEAD_GROUPS),
        in_specs=[blk(0), blk(1), blk(2),
                  pl.BlockSpec((NA_WIN_ROWS, HEAD_GROUP * GRID_W, BAND), lambda b, g: (0, g, 0))],
        out_specs=pl.BlockSpec((1, s, MXU_DIM), lambda b, g: (b, 0, g)),
        out_shape=jax.ShapeDtypeStruct((bsz, s, D_MODEL), BF16),
        compiler_params=_params("parallel", "parallel"),
        name="nbr_attention",
    )(qkv, qkv, qkv, bias)


def _proj_ln_kernel(y_ref, w_ref, x_ref, gate_ref, g_ref, b_ref, o_ref):
    y = jnp.dot(y_ref[0], w_ref[...], preferred_element_type=F32)
    h = ALPHA * x_ref[0] + (1.0 + gate_ref[0]) * y
    o_ref[0] = _layer_norm(h, g_ref[...], b_ref[...])


def _proj_ln(y, w, x, mod, ln_g, ln_b, tm=512):
    bsz, s, d = x.shape
    vec = pl.BlockSpec((1, d), lambda b, i: (0, 0))
    return pl.pallas_call(
        _proj_ln_kernel,
        grid=(bsz, s // tm),
        in_specs=[pl.BlockSpec((1, tm, d), lambda b, i: (b, i, 0)),
                  pl.BlockSpec((d, d), lambda b, i: (0, 0)),
                  pl.BlockSpec((1, tm, d), lambda b, i: (b, i, 0)),
                  _mod_spec(2), vec, vec],
        out_specs=pl.BlockSpec((1, tm, d), lambda b, i: (b, i, 0)),
        out_shape=jax.ShapeDtypeStruct((bsz, s, d), F32),
        compiler_params=_params("parallel", "parallel"),
        name="proj_ln",
    )(y, w, x, mod, ln_g.reshape(1, d), ln_b.reshape(1, d))


def _gelu_tanh(x):
    c = np.float32(np.sqrt(2.0 / np.pi))
    return x * (0.5 * (1.0 + jnp.tanh(c * (x + 0.044715 * (x * x * x)))))


def _ffn_kernel(x_ref, xp_ref, xn_ref, sh_ref, sc_ref, gate_ref, wup_ref, cw_ref, cb_ref, wd_ref,
                g_ref, b_ref, o_ref, u_ref, acc_ref, *, tm):
    i = pl.program_id(1)
    scale = 1.0 + sc_ref[0]
    shift = sh_ref[0]
    x = x_ref[0]
    prev = jnp.where(i > 0, xp_ref[0] * scale + shift, 0.0)
    nxt = jnp.where(i < pl.num_programs(1) - 1, xn_ref[0] * scale + shift, 0.0)
    u_ref[0:HALO, :] = prev.astype(BF16)
    u_ref[HALO:HALO + tm, :] = (x * scale + shift).astype(BF16)
    u_ref[HALO + tm:, :] = nxt.astype(BF16)
    acc_ref[...] = jnp.zeros_like(acc_ref)
    ext = tm + 2 * HALO
    cols = lambda c: slice(c * FF_CHUNK, (c + 1) * FF_CHUNK)

    def up_proj(c):
        a = jnp.dot(u_ref[...], wup_ref[:, cols(c)], preferred_element_type=F32)
        gt = jnp.dot(u_ref[HALO:HALO + tm, :], wup_ref[:, cols(N_FF_CHUNKS + c)], preferred_element_type=F32)
        return a, gt

    a, gt = up_proj(0)
    hidden = []
    for c in range(N_FF_CHUNKS):
        ahead = up_proj(c + 1) if c + 1 < N_FF_CHUNKS else None
        cw = cw_ref[:, cols(c)]
        a_prev = pltpu.roll(a, 1, 0)[HALO:HALO + tm]
        a_next = pltpu.roll(a, ext - 1, 0)[HALO:HALO + tm]
        conv = cb_ref[:, cols(c)] + a_prev * cw[0:1] + a[HALO:HALO + tm] * cw[1:2] + a_next * cw[2:3]
        hidden.append((_gelu_tanh(conv) * gt).astype(BF16))
        if len(hidden) == FF_DOWN_GROUP or ahead is None:
            first = c + 1 - len(hidden)
            wd = wd_ref[first * FF_CHUNK:(c + 1) * FF_CHUNK, :]
            acc_ref[...] += jnp.dot(jnp.concatenate(hidden, axis=1), wd, preferred_element_type=F32)
            hidden = []
        if ahead is not None:
            a, gt = ahead

    h = ALPHA * x + (1.0 + gate_ref[0]) * acc_ref[...]
    o_ref[0] = _layer_norm(h, g_ref[...], b_ref[...])


def _conv_ffn(x, mod, w_up, conv_w, conv_b, w_down, ln_g, ln_b, tm=512):
    bsz, s, d = x.shape
    nblk = s // tm
    hb = tm // HALO
    n_halo = s // HALO
    vec = pl.BlockSpec((1, d), lambda b, i: (0, 0))
    full = lambda a: pl.BlockSpec(a.shape, lambda b, i: (0, 0), pipeline_mode=pl.Buffered(1))
    wup = w_up.astype(BF16)
    wd = w_down.astype(BF16)
    cb = conv_b.reshape(1, D_FF)
    return pl.pallas_call(
        functools.partial(_ffn_kernel, tm=tm),
        grid=(bsz, nblk),
        in_specs=[pl.BlockSpec((1, tm, d), lambda b, i: (b, i, 0)),
                  pl.BlockSpec((1, HALO, d), lambda b, i: (b, jnp.maximum(i * hb - 1, 0), 0)),
                  pl.BlockSpec((1, HALO, d), lambda b, i: (b, jnp.minimum((i + 1) * hb, n_halo - 1), 0)),
                  _mod_spec(3), _mod_spec(4), _mod_spec(5),
                  full(wup), full(conv_w), full(cb), full(wd), vec, vec],
        out_specs=pl.BlockSpec((1, tm, d), lambda b, i: (b, i, 0)),
        out_shape=jax.ShapeDtypeStruct((bsz, s, d), F32),
        scratch_shapes=[pltpu.VMEM((tm + 2 * HALO, d), BF16), pltpu.VMEM((tm, d), F32)],
        compiler_params=_params("parallel", "arbitrary"),
        name="conv_ffn",
    )(x, x, x, mod, mod, mod, wup, conv_w, cb, wd, ln_g.reshape(1, d), ln_b.reshape(1, d))


FFT_N1 = 64
FFT_N2 = 64
FFT_ROWS = 512
FFT_UNROLL = 8
FFT_PITCH = 72


def _fourier_consts():
    s = FFT_N1 * FFT_N2
    c = np.arange(FN_GROUP_DIM)
    ang = 2.0 * np.pi * np.outer(c, c) / FN_GROUP_DIM
    blk = lambda m: np.kron(np.eye(MXU_DIM // FN_GROUP_DIM), m)
    scale = 1.0 / np.sqrt(s * FN_GROUP_DIM)
    chan = np.concatenate([blk(np.cos(ang)), -blk(np.sin(ang))], axis=1) * scale
    k1 = np.arange(FFT_N1)[None, :, None]
    a = np.arange(FFT_N1)[None, None, :]
    b = np.arange(FFT_N2)[:, None, None]
    th = 2.0 * np.pi * ((k1 * (FFT_N2 * a + b)) % s) / s
    st1 = np.concatenate([np.concatenate([np.cos(th), np.sin(th)], axis=2),
                          np.concatenate([-np.sin(th), np.cos(th)], axis=2)], axis=1)
    k2 = np.arange(FFT_N2)[:, None]
    bb = np.arange(FFT_N2)[None, :]
    ph = 2.0 * np.pi * ((k2 * bb) % FFT_N2) / FFT_N2
    st2 = np.concatenate([np.cos(ph), np.sin(ph)], axis=1)
    f32 = lambda m: jnp.asarray(m.astype(np.float32))
    return f32(chan).astype(BF16), f32(st1).astype(BF16), f32(st2).astype(BF16)


def _fourier_kernel(x_ref, sh_ref, sc_ref, chan_ref, st1_ref, st2_ref, o_ref, z_ref, y_ref):
    s = FFT_N1 * FFT_N2
    n_parts = 2 * MXU_DIM // LANES
    groups = FFT_ROWS // FFT_N2
    scale = 1.0 + sc_ref[0]
    shift = sh_ref[0]
    group_rows = lambda g: pl.ds(pl.multiple_of(g * FFT_PITCH, 8), FFT_N2)

    def chan_dft(j, carry):
        rows = pl.ds(pl.multiple_of(j * FFT_ROWS, FFT_ROWS), FFT_ROWS)
        u = (x_ref[0, rows, :] * scale + shift).astype(BF16)
        z = jnp.dot(u, chan_ref[...], preferred_element_type=F32)
        for i in range(groups):
            for p in range(n_parts):
                z_ref[p, group_rows(j * groups + i), :] = z[i * FFT_N2:(i + 1) * FFT_N2, p * LANES:(p + 1) * LANES]
        return carry

    lax.fori_loop(0, s // FFT_ROWS, chan_dft, 0, unroll=2)

    def gather(ref, r):
        take = pl.ds(r, FFT_N1, stride=FFT_PITCH)
        re = jnp.concatenate([ref[0, take, :], ref[1, take, :]], axis=1)
        im = jnp.concatenate([ref[2, take, :], ref[3, take, :]], axis=1)
        return jnp.concatenate([re, im], axis=0).astype(BF16)

    def stage1(b, carry):
        y = jnp.dot(st1_ref[b], gather(z_ref, b), preferred_element_type=F32)
        for p in range(n_parts):
            half, lane = divmod(p, 2)
            y_ref[p, group_rows(b), :] = y[half * FFT_N1:(half + 1) * FFT_N1, lane * LANES:(lane + 1) * LANES]
        return carry

    lax.fori_loop(0, FFT_N2, stage1, 0, unroll=FFT_UNROLL)

    def stage2(k1, carry):
        out = jnp.dot(st2_ref[...], gather(y_ref, k1), preferred_element_type=F32)
        for lane in range(MXU_DIM // LANES):
            z_ref[lane, pl.ds(k1, FFT_N2, stride=FFT_PITCH), :] = out[:, lane * LANES:(lane + 1) * LANES]
        return carry

    lax.fori_loop(0, FFT_N1, stage2, 0, unroll=FFT_UNROLL)
    for k2 in range(FFT_N2):
        src = slice(k2 * FFT_PITCH, k2 * FFT_PITCH + FFT_N1)
        o_ref[0, k2 * FFT_N1:(k2 + 1) * FFT_N1, :] = jnp.concatenate(
            [z_ref[0, src, :], z_ref[1, src, :]], axis=1).astype(BF16)


def _fourier_mix(x, mod):
    bsz, s, d = x.shape
    assert s == FFT_N1 * FFT_N2
    chan, st1, st2 = _fourier_consts()
    mspec = lambda piece: pl.BlockSpec((1, 1, MXU_DIM), lambda b, g: (b, 0, piece * (d // MXU_DIM) + g))
    buf = pltpu.VMEM((2 * MXU_DIM // LANES, FFT_N1 * FFT_PITCH, LANES), F32)
    return pl.pallas_call(
        _fourier_kernel,
        grid=(bsz, d // MXU_DIM),
        in_specs=[pl.BlockSpec((1, s, MXU_DIM), lambda b, g: (b, 0, g)),
                  mspec(0), mspec(1),
                  pl.BlockSpec(chan.shape, lambda b, g: (0, 0)),
                  pl.BlockSpec(st1.shape, lambda b, g: (0, 0, 0)),
                  pl.BlockSpec(st2.shape, lambda b, g: (0, 0))],
        out_specs=pl.BlockSpec((1, s, MXU_DIM), lambda b, g: (b, 0, g)),
        out_shape=jax.ShapeDtypeStruct((bsz, s, d), BF16),
        scratch_shapes=[buf, buf],
        compiler_params=_params("parallel", "parallel"),
        name="fourier_mix",
    )(x, mod, mod, chan, st1, st2)


def kernel(x, c, ada_w, ada_b, na_w_qkv, na_rpb, na_w_o, fn_w_o, ln1_g, ln1_b,
           ffn_w_up, ffn_conv_w, ffn_conv_b, ffn_w_down, ln2_g, ln2_b):
    bsz = x.shape[0]
    mod_all = _modulation(c, ada_w, ada_b)
    for i in range(DEPTH):
        mod = mod_all[i].reshape(bsz, 1, 6 * D_MODEL)
        j = i // 2
        if i % 2 == 0:
            qkv = _qkv_proj(x, mod, na_w_qkv[j].astype(BF16))
            y = _attention(qkv, _bias_table(na_rpb[j]))
            w_o = na_w_o[j]
        else:
            y = _fourier_mix(x, mod)
            w_o = fn_w_o[j]
        x = _proj_ln(y, w_o.astype(BF16), x, mod, ln1_g[i], ln1_b[i])
        x = _conv_ffn(x, mod, ffn_w_up[i], ffn_conv_w[i], ffn_conv_b[i], ffn_w_down[i], ln2_g[i], ln2_b[i])
    return x
```

```python
import functools

import numpy as np
import jax
import jax.numpy as jnp
from jax import lax
from jax.experimental import pallas as pl
from jax.experimental.pallas import tpu as pltpu

F32 = jnp.float32
BF16 = jnp.bfloat16

D_MODEL = 1024
DEPTH = 2
GRID_W = 64
NA_HEADS = 16
NA_HEAD_DIM = D_MODEL // NA_HEADS
NA_WIN_ROWS = 8
NA_WIN_COLS = 16
FN_GROUP_DIM = 128
D_FF = 2816
CONV_W = 3
LN_EPS = 1e-5
ALPHA = (2.0 * DEPTH) ** 0.25

LANES = 128
MXU_DIM = 256
VMEM_LIMIT = 56 * 1024 * 1024

HEAD_GROUP = MXU_DIM // NA_HEAD_DIM
N_HEAD_GROUPS = NA_HEADS // HEAD_GROUP
BAND = NA_WIN_ROWS * GRID_W
FF_CHUNK = MXU_DIM
N_FF_CHUNKS = D_FF // FF_CHUNK
HALO = 16
MASK_VALUE = -1e30
ATTN_UNROLL = 8


def _params(*sem):
    return pltpu.CompilerParams(dimension_semantics=sem, vmem_limit_bytes=VMEM_LIMIT)


def _layer_norm(h, g, b):
    mu = jnp.mean(h, axis=-1, keepdims=True)
    d = h - mu
    var = jnp.mean(d * d, axis=-1, keepdims=True)
    return d * lax.rsqrt(var + LN_EPS) * g + b


def _mod_kernel(c_ref, w_ref, b_ref, o_ref):
    c = c_ref[...]
    cs = c * jax.nn.sigmoid(c)
    o_ref[0] = jnp.dot(cs, w_ref[0], preferred_element_type=F32,
                       precision=lax.Precision.HIGHEST) + b_ref[0]


def _modulation(c, ada_w, ada_b):
    depth, d, n = ada_w.shape
    bsz = c.shape[0]
    tn = 1536
    return pl.pallas_call(
        _mod_kernel,
        grid=(depth, n // tn),
        in_specs=[pl.BlockSpec((bsz, d), lambda i, j: (0, 0)),
                  pl.BlockSpec((1, d, tn), lambda i, j: (i, 0, j)),
                  pl.BlockSpec((1, 1, tn), lambda i, j: (i, 0, j))],
        out_specs=pl.BlockSpec((1, bsz, tn), lambda i, j: (i, 0, j)),
        out_shape=jax.ShapeDtypeStruct((depth, bsz, n), F32),
        compiler_params=_params("parallel", "parallel"),
        name="adaln_mod",
    )(c, ada_w, ada_b.reshape(depth, 1, n))


def _mod_spec(piece):
    return pl.BlockSpec((1, 1, D_MODEL), lambda b, i: (b, 0, piece))


def _qkv_kernel(x_ref, sh_ref, sc_ref, w_ref, o_ref):
    u = (x_ref[0] * (1.0 + sc_ref[0]) + sh_ref[0]).astype(BF16)
    for j in range(3):
        cols = slice(j * D_MODEL, (j + 1) * D_MODEL)
        acc = jnp.dot(u, w_ref[:, cols], preferred_element_type=F32)
        if j == 0:
            acc = acc * (NA_HEAD_DIM ** -0.5)
        o_ref[0, :, cols] = acc.astype(BF16)


def _qkv_proj(x, mod, w_qkv, tm=512):
    bsz, s, d = x.shape
    return pl.pallas_call(
        _qkv_kernel,
        grid=(bsz, s // tm),
        in_specs=[pl.BlockSpec((1, tm, d), lambda b, i: (b, i, 0)),
                  _mod_spec(0), _mod_spec(1),
                  pl.BlockSpec((d, 3 * d), lambda b, i: (0, 0))],
        out_specs=pl.BlockSpec((1, tm, 3 * d), lambda b, i: (b, i, 0)),
        out_shape=jax.ShapeDtypeStruct((bsz, s, 3 * d), BF16),
        compiler_params=_params("parallel", "parallel"),
        name="qkv_proj",
    )(x, mod, mod, w_qkv)


def _bias_kernel(rpb_ref, o_ref):
    h = pl.program_id(0)
    n_dr = 2 * NA_WIN_ROWS - 1
    n_dc = 2 * NA_WIN_COLS - 1
    q = lax.broadcasted_iota(jnp.int32, (GRID_W, GRID_W), 0)
    k = lax.broadcasted_iota(jnp.int32, (GRID_W, GRID_W), 1)
    col_start = jnp.clip(q - NA_WIN_COLS // 2, 0, GRID_W - NA_WIN_COLS)
    col_in = (k >= col_start) & (k < col_start + NA_WIN_COLS)
    dc = jnp.clip(k - q + NA_WIN_COLS - 1, 0, n_dc - 1)
    tiles = []
    for dr in range(n_dr):
        t = jnp.full((GRID_W, GRID_W), MASK_VALUE, F32)
        for d in range(n_dc):
            val = rpb_ref[(h * n_dr + dr) * n_dc + d]
            t = jnp.where(col_in & (dc == d), val, t)
        tiles.append(t)
    for o in range(NA_WIN_ROWS):
        o_ref[o, 0] = jnp.concatenate(tiles[o:o + NA_WIN_ROWS], axis=1)


def _bias_table(rpb):
    bands = pl.pallas_call(
        _bias_kernel,
        grid=(NA_HEADS,),
        in_specs=[pl.BlockSpec(memory_space=pltpu.SMEM)],
        out_specs=pl.BlockSpec((NA_WIN_ROWS, 1, GRID_W, BAND), lambda h: (0, h, 0, 0)),
        out_shape=jax.ShapeDtypeStruct((NA_WIN_ROWS, NA_HEADS, GRID_W, BAND), F32),
        compiler_params=_params("parallel"),
        name="rpb_table",
    )(rpb.reshape(-1))
    return bands.reshape(NA_WIN_ROWS, NA_HEADS * GRID_W, BAND)


def _attn_kernel(q_ref, k_ref, v_ref, bias_ref, o_ref, *, rows):
    lane_head = lax.broadcasted_iota(jnp.int32, (GRID_W, MXU_DIM), 1) // NA_HEAD_DIM

    def one_row(r, carry):
        rs = jnp.clip(r - NA_WIN_ROWS // 2, 0, rows - NA_WIN_ROWS)
        off = rs - r + NA_WIN_ROWS - 1
        qr = q_ref[0, pl.ds(pl.multiple_of(r * GRID_W, GRID_W), GRID_W), :]
        zero = jnp.zeros_like(qr)
        q_heads = jnp.concatenate([jnp.where(lane_head == h, qr, zero) for h in range(HEAD_GROUP)], axis=0)
        band = pl.ds(pl.multiple_of(rs * GRID_W, GRID_W), BAND)
        kb = k_ref[0, band, :]
        s = lax.dot_general(q_heads, kb, (((1,), (1,)), ((), ())), preferred_element_type=F32)
        s = s + bias_ref[off]
        m = jnp.max(s, axis=-1, keepdims=True)
        p = jnp.exp(s - m)
        l = jnp.sum(p, axis=-1, keepdims=True)
        vb = v_ref[0, band, :]
        g = jnp.dot(p.astype(BF16), vb, preferred_element_type=F32) / l
        out = jnp.zeros((GRID_W, MXU_DIM), F32)
        for h in range(HEAD_GROUP):
            out = jnp.where(lane_head == h, g[h * GRID_W:(h + 1) * GRID_W, :], out)
        o_ref[0, pl.ds(pl.multiple_of(r * GRID_W, GRID_W), GRID_W), :] = out.astype(BF16)
        return carry

    lax.fori_loop(0, rows, one_row, 0, unroll=ATTN_UNROLL)


def _attention(qkv, bias):
    bsz, s, _ = qkv.shape
    rows = s // GRID_W
    assert rows >= NA_WIN_ROWS
    blk = lambda part: pl.BlockSpec((1, s, MXU_DIM), lambda b, g: (b, 0, part * N_HEAD_GROUPS + g))
    return pl.pallas_call(
        functools.partial(_attn_kernel, rows=rows),
        grid=(bsz, N_HEAD_GROUPS),
        in_specs=[blk(0), blk(1), blk(2),
                  pl.BlockSpec((NA_WIN_ROWS, HEAD_GROUP * GRID_W, BAND), lambda b, g: (0, g, 0))],
        out_specs=pl.BlockSpec((1, s, MXU_DIM), lambda b, g: (b, 0, g)),
        out_shape=jax.ShapeDtypeStruct((bsz, s, D_MODEL), BF16),
        compiler_params=_params("parallel", "parallel"),
        name="nbr_attention",
    )(qkv, qkv, qkv, bias)


def _proj_ln_kernel(y_ref, w_ref, x_ref, gate_ref, g_ref, b_ref, o_ref):
    y = jnp.dot(y_ref[0], w_ref[...], preferred_element_type=F32)
    h = ALPHA * x_ref[0] + (1.0 + gate_ref[0]) * y
    o_ref[0] = _layer_norm(h, g_ref[...], b_ref[...])


def _proj_ln(y, w, x, mod, ln_g, ln_b, tm=1024):
    bsz, s, d = x.shape
    vec = pl.BlockSpec((1, d), lambda b, i: (0, 0))
    return pl.pallas_call(
        _proj_ln_kernel,
        grid=(bsz, s // tm),
        in_specs=[pl.BlockSpec((1, tm, d), lambda b, i: (b, i, 0)),
                  pl.BlockSpec((d, d), lambda b, i: (0, 0)),
                  pl.BlockSpec((1, tm, d), lambda b, i: (b, i, 0)),
                  _mod_spec(2), vec, vec],
        out_specs=pl.BlockSpec((1, tm, d), lambda b, i: (b, i, 0)),
        out_shape=jax.ShapeDtypeStruct((bsz, s, d), F32),
        compiler_params=_params("parallel", "parallel"),
        name="proj_ln",
    )(y, w, x, mod, ln_g.reshape(1, d), ln_b.reshape(1, d))


def _gelu_tanh(x):
    c = np.float32(np.sqrt(2.0 / np.pi))
    return x * (0.5 * (1.0 + jnp.tanh(c * (x + 0.044715 * (x * x * x)))))


def _ffn_kernel(x_ref, xp_ref, xn_ref, sh_ref, sc_ref, gate_ref, wup_ref, cw_ref, cb_ref, wd_ref,
                g_ref, b_ref, o_ref, u_ref, *, tm):
    i = pl.program_id(1)
    scale = 1.0 + sc_ref[0]
    shift = sh_ref[0]
    x = x_ref[0]
    prev = jnp.where(i > 0, xp_ref[0] * scale + shift, 0.0)
    nxt = jnp.where(i < pl.num_programs(1) - 1, xn_ref[0] * scale + shift, 0.0)
    u_ref[0:HALO, :] = prev.astype(BF16)
    u_ref[HALO:HALO + tm, :] = (x * scale + shift).astype(BF16)
    u_ref[HALO + tm:, :] = nxt.astype(BF16)
    ext = tm + 2 * HALO
    cols = lambda c: slice(c * FF_CHUNK, (c + 1) * FF_CHUNK)

    def up_proj(c):
        a = jnp.dot(u_ref[...], wup_ref[:, cols(c)], preferred_element_type=F32)
        gt = jnp.dot(u_ref[HALO:HALO + tm, :], wup_ref[:, cols(N_FF_CHUNKS + c)], preferred_element_type=F32)
        return a, gt

    a, gt = up_proj(0)
    hidden = []
    for c in range(N_FF_CHUNKS):
        ahead = up_proj(c + 1) if c + 1 < N_FF_CHUNKS else None
        cw = cw_ref[:, cols(c)]
        a_prev = pltpu.roll(a, 1, 0)[HALO:HALO + tm]
        a_next = pltpu.roll(a, ext - 1, 0)[HALO:HALO + tm]
        conv = cb_ref[:, cols(c)] + a_prev * cw[0:1] + a[HALO:HALO + tm] * cw[1:2] + a_next * cw[2:3]
        hidden.append((_gelu_tanh(conv) * gt).astype(BF16))
        if ahead is not None:
            a, gt = ahead

    y = jnp.dot(jnp.concatenate(hidden, axis=1), wd_ref[...], preferred_element_type=F32)
    h = ALPHA * x + (1.0 + gate_ref[0]) * y
    o_ref[0] = _layer_norm(h, g_ref[...], b_ref[...])


def _conv_ffn(x, mod, w_up, conv_w, conv_b, w_down, ln_g, ln_b, tm=512):
    bsz, s, d = x.shape
    nblk = s // tm
    hb = tm // HALO
    n_halo = s // HALO
    vec = pl.BlockSpec((1, d), lambda b, i: (0, 0))
    full = lambda a: pl.BlockSpec(a.shape, lambda b, i: (0, 0), pipeline_mode=pl.Buffered(1))
    wup = w_up.astype(BF16)
    wd = w_down.astype(BF16)
    cb = conv_b.reshape(1, D_FF)
    return pl.pallas_call(
        functools.partial(_ffn_kernel, tm=tm),
        grid=(bsz, nblk),
        in_specs=[pl.BlockSpec((1, tm, d), lambda b, i: (b, i, 0)),
                  pl.BlockSpec((1, HALO, d), lambda b, i: (b, jnp.maximum(i * hb - 1, 0), 0)),
                  pl.BlockSpec((1, HALO, d), lambda b, i: (b, jnp.minimum((i + 1) * hb, n_halo - 1), 0)),
                  _mod_spec(3), _mod_spec(4), _mod_spec(5),
                  full(wup), full(conv_w), full(cb), full(wd), vec, vec],
        out_specs=pl.BlockSpec((1, tm, d), lambda b, i: (b, i, 0)),
        out_shape=jax.ShapeDtypeStruct((bsz, s, d), F32),
        scratch_shapes=[pltpu.VMEM((tm + 2 * HALO, d), BF16)],
        compiler_params=_params("parallel", "arbitrary"),
        name="conv_ffn",
    )(x, x, x, mod, mod, mod, wup, conv_w, cb, wd, ln_g.reshape(1, d), ln_b.reshape(1, d))


FFT_N1 = 64
FFT_N2 = 64
FFT_ROWS = 512
FFT_UNROLL = 16
FFT_PITCH = 72


def _fourier_consts():
    s = FFT_N1 * FFT_N2
    c = np.arange(FN_GROUP_DIM)
    ang = 2.0 * np.pi * np.outer(c, c) / FN_GROUP_DIM
    blk = lambda m: np.kron(np.eye(MXU_DIM // FN_GROUP_DIM), m)
    scale = 1.0 / np.sqrt(s * FN_GROUP_DIM)
    chan = np.concatenate([blk(np.cos(ang)), -blk(np.sin(ang))], axis=1) * scale
    k1 = np.arange(FFT_N1)[None, :, None]
    a = np.arange(FFT_N1)[None, None, :]
    b = np.arange(FFT_N2)[:, None, None]
    th = 2.0 * np.pi * ((k1 * (FFT_N2 * a + b)) % s) / s
    st1 = np.concatenate([np.concatenate([np.cos(th), np.sin(th)], axis=2),
                          np.concatenate([-np.sin(th), np.cos(th)], axis=2)], axis=1)
    k2 = np.arange(FFT_N2)[:, None]
    bb = np.arange(FFT_N2)[None, :]
    ph = 2.0 * np.pi * ((k2 * bb) % FFT_N2) / FFT_N2
    st2 = np.concatenate([np.cos(ph), np.sin(ph)], axis=1)
    f32 = lambda m: jnp.asarray(m.astype(np.float32))
    return f32(chan).astype(BF16), f32(st1).astype(BF16), f32(st2).astype(BF16)


def _fourier_kernel(x_ref, sh_ref, sc_ref, chan_ref, st1_ref, st2_ref, o_ref, z_ref, y_ref):
    s = FFT_N1 * FFT_N2
    n_parts = 2 * MXU_DIM // LANES
    groups = FFT_ROWS // FFT_N2
    scale = 1.0 + sc_ref[0]
    shift = sh_ref[0]
    group_rows = lambda g: pl.ds(pl.multiple_of(g * FFT_PITCH, 8), FFT_N2)

    def chan_dft(j, carry):
        rows = pl.ds(pl.multiple_of(j * FFT_ROWS, FFT_ROWS), FFT_ROWS)
        u = (x_ref[0, rows, :] * scale + shift).astype(BF16)
        z = jnp.dot(u, chan_ref[...], preferred_element_type=F32)
        for i in range(groups):
            for p in range(n_parts):
                z_ref[p, group_rows(j * groups + i), :] = z[i * FFT_N2:(i + 1) * FFT_N2, p * LANES:(p + 1) * LANES]
        return carry

    lax.fori_loop(0, s // FFT_ROWS, chan_dft, 0, unroll=2)

    def gather(ref, r):
        take = pl.ds(r, FFT_N1, stride=FFT_PITCH)
        re = jnp.concatenate([ref[0, take, :], ref[1, take, :]], axis=1)
        im = jnp.concatenate([ref[2, take, :], ref[3, take, :]], axis=1)
        return jnp.concatenate([re, im], axis=0).astype(BF16)

    def stage1(b, carry):
        y = jnp.dot(st1_ref[b], gather(z_ref, b), preferred_element_type=F32)
        for p in range(n_parts):
            half, lane = divmod(p, 2)
            y_ref[p, group_rows(b), :] = y[half * FFT_N1:(half + 1) * FFT_N1, lane * LANES:(lane + 1) * LANES]
        return carry

    lax.fori_loop(0, FFT_N2, stage1, 0, unroll=FFT_UNROLL)

    def stage2(k1, carry):
        out = jnp.dot(st2_ref[...], gather(y_ref, k1), preferred_element_type=F32)
        for lane in range(MXU_DIM // LANES):
            z_ref[lane, pl.ds(k1, FFT_N2, stride=FFT_PITCH), :] = out[:, lane * LANES:(lane + 1) * LANES]
        return carry

    lax.fori_loop(0, FFT_N1, stage2, 0, unroll=FFT_UNROLL)
    for k2 in range(FFT_N2):
        src = slice(k2 * FFT_PITCH, k2 * FFT_PITCH + FFT_N1)
        o_ref[0, k2 * FFT_N1:(k2 + 1) * FFT_N1, :] = jnp.concatenate(
            [z_ref[0, src, :], z_ref[1, src, :]], axis=1).astype(BF16)


def _fourier_mix(x, mod):
    bsz, s, d = x.shape
    assert s == FFT_N1 * FFT_N2
    chan, st1, st2 = _fourier_consts()
    mspec = lambda piece: pl.BlockSpec((1, 1, MXU_DIM), lambda b, g: (b, 0, piece * (d // MXU_DIM) + g))
    buf = pltpu.VMEM((2 * MXU_DIM // LANES, FFT_N1 * FFT_PITCH, LANES), F32)
    return pl.pallas_call(
        _fourier_kernel,
        grid=(bsz, d // MXU_DIM),
        in_specs=[pl.BlockSpec((1, s, MXU_DIM), lambda b, g: (b, 0, g)),
                  mspec(0), mspec(1),
                  pl.BlockSpec(chan.shape, lambda b, g: (0, 0)),
                  pl.BlockSpec(st1.shape, lambda b, g: (0, 0, 0)),
                  pl.BlockSpec(st2.shape, lambda b, g: (0, 0))],
        out_specs=pl.BlockSpec((1, s, MXU_DIM), lambda b, g: (b, 0, g)),
        out_shape=jax.ShapeDtypeStruct((bsz, s, d), BF16),
        scratch_shapes=[buf, buf],
        compiler_params=_params("parallel", "parallel"),
        name="fourier_mix",
    )(x, mod, mod, chan, st1, st2)


def kernel(x, c, ada_w, ada_b, na_w_qkv, na_rpb, na_w_o, fn_w_o, ln1_g, ln1_b,
           ffn_w_up, ffn_conv_w, ffn_conv_b, ffn_w_down, ln2_g, ln2_b):
    bsz = x.shape[0]
    mod_all = _modulation(c, ada_w, ada_b)
    for i in range(DEPTH):
        mod = mod_all[i].reshape(bsz, 1, 6 * D_MODEL)
        j = i // 2
        if i % 2 == 0:
            qkv = _qkv_proj(x, mod, na_w_qkv[j].astype(BF16))
            y = _attention(qkv, _bias_table(na_rpb[j]))
            w_o = na_w_o[j]
        else:
            y = _fourier_mix(x, mod)
            w_o = fn_w_o[j]
        x = _proj_ln(y, w_o.astype(BF16), x, mod, ln1_g[i], ln1_b[i])
        x = _conv_ffn(x, mod, ffn_w_up[i], ffn_conv_w[i], ffn_conv_b[i], ffn_w_down[i], ln2_g[i], ln2_b[i])
    return x
```

```python
import functools

import numpy as np
import jax
import jax.numpy as jnp
from jax import lax
from jax.experimental import pallas as pl
from jax.experimental.pallas import tpu as pltpu

F32 = jnp.float32
BF16 = jnp.bfloat16

D_MODEL = 1024
DEPTH = 2
GRID_W = 64
NA_HEADS = 16
NA_HEAD_DIM = D_MODEL // NA_HEADS
NA_WIN_ROWS = 8
NA_WIN_COLS = 16
FN_GROUP_DIM = 128
D_FF = 2816
CONV_W = 3
LN_EPS = 1e-5
ALPHA = (2.0 * DEPTH) ** 0.25

LANES = 128
MXU_DIM = 256
VMEM_LIMIT = 56 * 1024 * 1024

HEAD_GROUP = MXU_DIM // NA_HEAD_DIM
N_HEAD_GROUPS = NA_HEADS // HEAD_GROUP
BAND = NA_WIN_ROWS * GRID_W
FF_CHUNK = MXU_DIM
N_FF_CHUNKS = D_FF // FF_CHUNK
HALO = 16
EPILOGUE_ROWS = 256
MASK_VALUE = -1e30
ATTN_UNROLL = 16


def _params(*sem):
    return pltpu.CompilerParams(dimension_semantics=sem, vmem_limit_bytes=VMEM_LIMIT)


def _layer_norm(h, g, b):
    mu = jnp.mean(h, axis=-1, keepdims=True)
    d = h - mu
    var = jnp.mean(d * d, axis=-1, keepdims=True)
    return d * lax.rsqrt(var + LN_EPS) * g + b


def _mod_kernel(c_ref, w_ref, b_ref, o_ref):
    c = c_ref[...]
    cs = c * jax.nn.sigmoid(c)
    o_ref[0] = jnp.dot(cs, w_ref[0], preferred_element_type=F32,
                       precision=lax.Precision.HIGHEST) + b_ref[0]


def _modulation(c, ada_w, ada_b):
    depth, d, n = ada_w.shape
    bsz = c.shape[0]
    tn = 1536
    return pl.pallas_call(
        _mod_kernel,
        grid=(depth, n // tn),
        in_specs=[pl.BlockSpec((bsz, d), lambda i, j: (0, 0)),
                  pl.BlockSpec((1, d, tn), lambda i, j: (i, 0, j)),
                  pl.BlockSpec((1, 1, tn), lambda i, j: (i, 0, j))],
        out_specs=pl.BlockSpec((1, bsz, tn), lambda i, j: (i, 0, j)),
        out_shape=jax.ShapeDtypeStruct((depth, bsz, n), F32),
        compiler_params=_params("parallel", "parallel"),
        name="adaln_mod",
    )(c, ada_w, ada_b.reshape(depth, 1, n))


def _mod_spec(piece):
    return pl.BlockSpec((1, 1, D_MODEL), lambda b, i: (b, 0, piece))


def _qkv_kernel(x_ref, sh_ref, sc_ref, w_ref, o_ref):
    u = (x_ref[0] * (1.0 + sc_ref[0]) + sh_ref[0]).astype(BF16)
    for j in range(3):
        cols = slice(j * D_MODEL, (j + 1) * D_MODEL)
        acc = jnp.dot(u, w_ref[:, cols], preferred_element_type=F32)
        if j == 0:
            acc = acc * (NA_HEAD_DIM ** -0.5)
        o_ref[0, :, cols] = acc.astype(BF16)


def _qkv_proj(x, mod, w_qkv, tm=512):
    bsz, s, d = x.shape
    return pl.pallas_call(
        _qkv_kernel,
        grid=(bsz, s // tm),
        in_specs=[pl.BlockSpec((1, tm, d), lambda b, i: (b, i, 0)),
                  _mod_spec(0), _mod_spec(1),
                  pl.BlockSpec((d, 3 * d), lambda b, i: (0, 0))],
        out_specs=pl.BlockSpec((1, tm, 3 * d), lambda b, i: (b, i, 0)),
        out_shape=jax.ShapeDtypeStruct((bsz, s, 3 * d), BF16),
        compiler_params=_params("parallel", "parallel"),
        name="qkv_proj",
    )(x, mod, mod, w_qkv)


def _bias_kernel(rpb_ref, o_ref):
    h = pl.program_id(0)
    n_dr = 2 * NA_WIN_ROWS - 1
    n_dc = 2 * NA_WIN_COLS - 1
    q = lax.broadcasted_iota(jnp.int32, (GRID_W, GRID_W), 0)
    k = lax.broadcasted_iota(jnp.int32, (GRID_W, GRID_W), 1)
    col_start = jnp.clip(q - NA_WIN_COLS // 2, 0, GRID_W - NA_WIN_COLS)
    col_in = (k >= col_start) & (k < col_start + NA_WIN_COLS)
    dc = jnp.clip(k - q + NA_WIN_COLS - 1, 0, n_dc - 1)
    tiles = []
    for dr in range(n_dr):
        t = jnp.full((GRID_W, GRID_W), MASK_VALUE, F32)
        for d in range(n_dc):
            val = rpb_ref[(h * n_dr + dr) * n_dc + d]
            t = jnp.where(col_in & (dc == d), val, t)
        tiles.append(t)
    for o in range(NA_WIN_ROWS):
        o_ref[o, 0] = jnp.concatenate(tiles[o:o + NA_WIN_ROWS], axis=1)


def _bias_table(rpb):
    bands = pl.pallas_call(
        _bias_kernel,
        grid=(NA_HEADS,),
        in_specs=[pl.BlockSpec(memory_space=pltpu.SMEM)],
        out_specs=pl.BlockSpec((NA_WIN_ROWS, 1, GRID_W, BAND), lambda h: (0, h, 0, 0)),
        out_shape=jax.ShapeDtypeStruct((NA_WIN_ROWS, NA_HEADS, GRID_W, BAND), F32),
        compiler_params=_params("parallel"),
        name="rpb_table",
    )(rpb.reshape(-1))
    return bands.reshape(NA_WIN_ROWS, NA_HEADS * GRID_W, BAND)


def _attn_kernel(q_ref, k_ref, v_ref, bias_ref, o_ref, *, rows):
    lane_head = lax.broadcasted_iota(jnp.int32, (GRID_W, MXU_DIM), 1) // NA_HEAD_DIM

    def one_row(r, carry):
        rs = jnp.clip(r - NA_WIN_ROWS // 2, 0, rows - NA_WIN_ROWS)
        off = rs - r + NA_WIN_ROWS - 1
        qr = q_ref[0, pl.ds(pl.multiple_of(r * GRID_W, GRID_W), GRID_W), :]
        zero = jnp.zeros_like(qr)
        q_heads = jnp.concatenate([jnp.where(lane_head == h, qr, zero) for h in range(HEAD_GROUP)], axis=0)
        band = pl.ds(pl.multiple_of(rs * GRID_W, GRID_W), BAND)
        kb = k_ref[0, band, :]
        s = lax.dot_general(q_heads, kb, (((1,), (1,)), ((), ())), preferred_element_type=F32)
        s = s + bias_ref[off]
        m = jnp.max(s, axis=-1, keepdims=True)
        p = jnp.exp(s - m)
        l = jnp.sum(p, axis=-1, keepdims=True)
        vb = v_ref[0, band, :]
        g = jnp.dot(p.astype(BF16), vb, preferred_element_type=F32) / l
        out = jnp.zeros((GRID_W, MXU_DIM), F32)
        for h in range(HEAD_GROUP):
            out = jnp.where(lane_head == h, g[h * GRID_W:(h + 1) * GRID_W, :], out)
        o_ref[0, pl.ds(pl.multiple_of(r * GRID_W, GRID_W), GRID_W), :] = out.astype(BF16)
        return carry

    lax.fori_loop(0, rows, one_row, 0, unroll=ATTN_UNROLL)


def _attention(qkv, bias):
    bsz, s, _ = qkv.shape
    rows = s // GRID_W
    assert rows >= NA_WIN_ROWS
    blk = lambda part: pl.BlockSpec((1, s, MXU_DIM), lambda b, g: (b, 0, part * N_HEAD_GROUPS + g))
    return pl.pallas_call(
        functools.partial(_attn_kernel, rows=rows),
        grid=(bsz, N_HEAD_GROUPS),
        in_specs=[blk(0), blk(1), blk(2),
                  pl.BlockSpec((NA_WIN_ROWS, HEAD_GROUP * GRID_W, BAND), lambda b, g: (0, g, 0))],
        out_specs=pl.BlockSpec((1, s, MXU_DIM), lambda b, g: (b, 0, g)),
        out_shape=jax.ShapeDtypeStruct((bsz, s, D_MODEL), BF16),
        compiler_params=_params("parallel", "parallel"),
        name="nbr_attention",
    )(qkv, qkv, qkv, bias)


def _proj_ln_kernel(y_ref, w_ref, x_ref, gate_ref, g_ref, b_ref, o_ref):
    for j in range(y_ref.shape[1] // EPILOGUE_ROWS):
        rows = slice(j * EPILOGUE_ROWS, (j + 1) * EPILOGUE_ROWS)
        y = jnp.dot(y_ref[0, rows, :], w_ref[...], preferred_element_type=F32)
        h = ALPHA * x_ref[0, rows, :] + (1.0 + gate_ref[0]) * y
        o_ref[0, rows, :] = _layer_norm(h, g_ref[...], b_ref[...])


def _proj_ln(y, w, x, mod, ln_g, ln_b, tm=1024):
    bsz, s, d = x.shape
    vec = pl.BlockSpec((1, d), lambda b, i: (0, 0))
    return pl.pallas_call(
        _proj_ln_kernel,
        grid=(bsz, s // tm),
        in_specs=[pl.BlockSpec((1, tm, d), lambda b, i: (b, i, 0)),
                  pl.BlockSpec((d, d), lambda b, i: (0, 0)),
                  pl.BlockSpec((1, tm, d), lambda b, i: (b, i, 0)),
                  _mod_spec(2), vec, vec],
        out_specs=pl.BlockSpec((1, tm, d), lambda b, i: (b, i, 0)),
        out_shape=jax.ShapeDtypeStruct((bsz, s, d), F32),
        compiler_params=_params("parallel", "parallel"),
        name="proj_ln",
    )(y, w, x, mod, ln_g.reshape(1, d), ln_b.reshape(1, d))


def _gelu_tanh(x):
    c = np.float32(np.sqrt(2.0 / np.pi))
    return x * (0.5 * (1.0 + jnp.tanh(c * (x + 0.044715 * (x * x * x)))))


def _ffn_kernel(x_ref, xp_ref, xn_ref, sh_ref, sc_ref, gate_ref, wup_ref, cw_ref, cb_ref, wd_ref,
                g_ref, b_ref, o_ref, u_ref, *, tm):
    i = pl.program_id(1)
    scale = 1.0 + sc_ref[0]
    shift = sh_ref[0]
    x = x_ref[0]
    prev = jnp.where(i > 0, xp_ref[0] * scale + shift, 0.0)
    nxt = jnp.where(i < pl.num_programs(1) - 1, xn_ref[0] * scale + shift, 0.0)
    u_ref[0:HALO, :] = prev.astype(BF16)
    u_ref[HALO:HALO + tm, :] = (x * scale + shift).astype(BF16)
    u_ref[HALO + tm:, :] = nxt.astype(BF16)
    ext = tm + 2 * HALO
    cols = lambda c: slice(c * FF_CHUNK, (c + 1) * FF_CHUNK)

    def up_proj(c):
        a = jnp.dot(u_ref[...], wup_ref[:, cols(c)], preferred_element_type=F32)
        gt = jnp.dot(u_ref[HALO:HALO + tm, :], wup_ref[:, cols(N_FF_CHUNKS + c)], preferred_element_type=F32)
        return a, gt

    a, gt = up_proj(0)
    hidden = []
    for c in range(N_FF_CHUNKS):
        ahead = up_proj(c + 1) if c + 1 < N_FF_CHUNKS else None
        cw = cw_ref[:, cols(c)]
        a_prev = pltpu.roll(a, 1, 0)[HALO:HALO + tm]
        a_next = pltpu.roll(a, ext - 1, 0)[HALO:HALO + tm]
        conv = cb_ref[:, cols(c)] + a_prev * cw[0:1] + a[HALO:HALO + tm] * cw[1:2] + a_next * cw[2:3]
        hidden.append((_gelu_tanh(conv) * gt).astype(BF16))
        if ahead is not None:
            a, gt = ahead

    hid = jnp.concatenate(hidden, axis=1)
    for j in range(tm // EPILOGUE_ROWS):
        rows = slice(j * EPILOGUE_ROWS, (j + 1) * EPILOGUE_ROWS)
        y = jnp.dot(hid[rows], wd_ref[...], preferred_element_type=F32)
        h = ALPHA * x[rows] + (1.0 + gate_ref[0]) * y
        o_ref[0, rows, :] = _layer_norm(h, g_ref[...], b_ref[...])


def _conv_ffn(x, mod, w_up, conv_w, conv_b, w_down, ln_g, ln_b, tm=512):
    bsz, s, d = x.shape
    nblk = s // tm
    hb = tm // HALO
    n_halo = s // HALO
    vec = pl.BlockSpec((1, d), lambda b, i: (0, 0))
    full = lambda a: pl.BlockSpec(a.shape, lambda b, i: (0, 0), pipeline_mode=pl.Buffered(1))
    wup = w_up.astype(BF16)
    wd = w_down.astype(BF16)
    cb = conv_b.reshape(1, D_FF)
    return pl.pallas_call(
        functools.partial(_ffn_kernel, tm=tm),
        grid=(bsz, nblk),
        in_specs=[pl.BlockSpec((1, tm, d), lambda b, i: (b, i, 0)),
                  pl.BlockSpec((1, HALO, d), lambda b, i: (b, jnp.maximum(i * hb - 1, 0), 0)),
                  pl.BlockSpec((1, HALO, d), lambda b, i: (b, jnp.minimum((i + 1) * hb, n_halo - 1), 0)),
                  _mod_spec(3), _mod_spec(4), _mod_spec(5),
                  full(wup), full(conv_w), full(cb), full(wd), vec, vec],
        out_specs=pl.BlockSpec((1, tm, d), lambda b, i: (b, i, 0)),
        out_shape=jax.ShapeDtypeStruct((bsz, s, d), F32),
        scratch_shapes=[pltpu.VMEM((tm + 2 * HALO, d), BF16)],
        compiler_params=_params("parallel", "arbitrary"),
        name="conv_ffn",
    )(x, x, x, mod, mod, mod, wup, conv_w, cb, wd, ln_g.reshape(1, d), ln_b.reshape(1, d))


FFT_N1 = 64
FFT_N2 = 64
FFT_ROWS = 512
FFT_UNROLL = 16
FFT_PITCH = 72


def _fourier_consts():
    s = FFT_N1 * FFT_N2
    c = np.arange(FN_GROUP_DIM)
    ang = 2.0 * np.pi * np.outer(c, c) / FN_GROUP_DIM
    blk = lambda m: np.kron(np.eye(MXU_DIM // FN_GROUP_DIM), m)
    scale = 1.0 / np.sqrt(s * FN_GROUP_DIM)
    chan = np.concatenate([blk(np.cos(ang)), -blk(np.sin(ang))], axis=1) * scale
    k1 = np.arange(FFT_N1)[None, :, None]
    a = np.arange(FFT_N1)[None, None, :]
    b = np.arange(FFT_N2)[:, None, None]
    th = 2.0 * np.pi * ((k1 * (FFT_N2 * a + b)) % s) / s
    st1 = np.concatenate([np.concatenate([np.cos(th), np.sin(th)], axis=2),
                          np.concatenate([-np.sin(th), np.cos(th)], axis=2)], axis=1)
    k2 = np.arange(FFT_N2)[:, None]
    bb = np.arange(FFT_N2)[None, :]
    ph = 2.0 * np.pi * ((k2 * bb) % FFT_N2) / FFT_N2
    st2 = np.concatenate([np.cos(ph), np.sin(ph)], axis=1)
    f32 = lambda m: jnp.asarray(m.astype(np.float32))
    return f32(chan).astype(BF16), f32(st1).astype(BF16), f32(st2).astype(BF16)


def _fourier_kernel(x_ref, sh_ref, sc_ref, chan_ref, st1_ref, st2_ref, o_ref, z_ref, y_ref):
    s = FFT_N1 * FFT_N2
    n_parts = 2 * MXU_DIM // LANES
    groups = FFT_ROWS // FFT_N2
    scale = 1.0 + sc_ref[0]
    shift = sh_ref[0]
    group_rows = lambda g: pl.ds(pl.multiple_of(g * FFT_PITCH, 8), FFT_N2)

    def chan_dft(j, carry):
        rows = pl.ds(pl.multiple_of(j * FFT_ROWS, FFT_ROWS), FFT_ROWS)
        u = (x_ref[0, rows, :] * scale + shift).astype(BF16)
        z = jnp.dot(u, chan_ref[...], preferred_element_type=F32)
        for i in range(groups):
            for p in range(n_parts):
                z_ref[p, group_rows(j * groups + i), :] = z[i * FFT_N2:(i + 1) * FFT_N2, p * LANES:(p + 1) * LANES]
        return carry

    lax.fori_loop(0, s // FFT_ROWS, chan_dft, 0, unroll=2)

    def gather(ref, r):
        take = pl.ds(r, FFT_N1, stride=FFT_PITCH)
        re = jnp.concatenate([ref[0, take, :], ref[1, take, :]], axis=1)
        im = jnp.concatenate([ref[2, take, :], ref[3, take, :]], axis=1)
        return jnp.concatenate([re, im], axis=0).astype(BF16)

    def stage1(b, carry):
        y = jnp.dot(st1_ref[b], gather(z_ref, b), preferred_element_type=F32)
        for p in range(n_parts):
            half, lane = divmod(p, 2)
            y_ref[p, group_rows(b), :] = y[half * FFT_N1:(half + 1) * FFT_N1, lane * LANES:(lane + 1) * LANES]
        return carry

    lax.fori_loop(0, FFT_N2, stage1, 0, unroll=FFT_UNROLL)

    def stage2(k1, carry):
        out = jnp.dot(st2_ref[...], gather(y_ref, k1), preferred_element_type=F32)
        for lane in range(MXU_DIM // LANES):
            z_ref[lane, pl.ds(k1, FFT_N2, stride=FFT_PITCH), :] = out[:, lane * LANES:(lane + 1) * LANES]
        return carry

    lax.fori_loop(0, FFT_N1, stage2, 0, unroll=FFT_UNROLL)
    for k2 in range(FFT_N2):
        src = slice(k2 * FFT_PITCH, k2 * FFT_PITCH + FFT_N1)
        o_ref[0, k2 * FFT_N1:(k2 + 1) * FFT_N1, :] = jnp.concatenate(
            [z_ref[0, src, :], z_ref[1, src, :]], axis=1).astype(BF16)


def _fourier_mix(x, mod):
    bsz, s, d = x.shape
    assert s == FFT_N1 * FFT_N2
    chan, st1, st2 = _fourier_consts()
    mspec = lambda piece: pl.BlockSpec((1, 1, MXU_DIM), lambda b, g: (b, 0, piece * (d // MXU_DIM) + g))
    buf = pltpu.VMEM((2 * MXU_DIM // LANES, FFT_N1 * FFT_PITCH, LANES), F32)
    return pl.pallas_call(
        _fourier_kernel,
        grid=(bsz, d // MXU_DIM),
        in_specs=[pl.BlockSpec((1, s, MXU_DIM), lambda b, g: (b, 0, g)),
                  mspec(0), mspec(1),
                  pl.BlockSpec(chan.shape, lambda b, g: (0, 0)),
                  pl.BlockSpec(st1.shape, lambda b, g: (0, 0, 0)),
                  pl.BlockSpec(st2.shape, lambda b, g: (0, 0))],
        out_specs=pl.BlockSpec((1, s, MXU_DIM), lambda b, g: (b, 0, g)),
        out_shape=jax.ShapeDtypeStruct((bsz, s, d), BF16),
        scratch_shapes=[buf, buf],
        compiler_params=_params("parallel", "parallel"),
        name="fourier_mix",
    )(x, mod, mod, chan, st1, st2)


def kernel(x, c, ada_w, ada_b, na_w_qkv, na_rpb, na_w_o, fn_w_o, ln1_g, ln1_b,
           ffn_w_up, ffn_conv_w, ffn_conv_b, ffn_w_down, ln2_g, ln2_b):
    bsz = x.shape[0]
    mod_all = _modulation(c, ada_w, ada_b)
    for i in range(DEPTH):
        mod = mod_all[i].reshape(bsz, 1, 6 * D_MODEL)
        j = i // 2
        if i % 2 == 0:
            qkv = _qkv_proj(x, mod, na_w_qkv[j].astype(BF16))
            y = _attention(qkv, _bias_table(na_rpb[j]))
            w_o = na_w_o[j]
        else:
            y = _fourier_mix(x, mod)
            w_o = fn_w_o[j]
        x = _proj_ln(y, w_o.astype(BF16), x, mod, ln1_g[i], ln1_b[i])
        x = _conv_ffn(x, mod, ffn_w_up[i], ffn_conv_w[i], ffn_conv_b[i], ffn_w_down[i], ln2_g[i], ln2_b[i])
    return x
```

```python
import functools

import numpy as np
import jax
import jax.numpy as jnp
from jax import lax
from jax.experimental import pallas as pl
from jax.experimental.pallas import tpu as pltpu

F32 = jnp.float32
BF16 = jnp.bfloat16

D_MODEL = 1024
DEPTH = 2
GRID_W = 64
NA_HEADS = 16
NA_HEAD_DIM = D_MODEL // NA_HEADS
NA_WIN_ROWS = 8
NA_WIN_COLS = 16
FN_GROUP_DIM = 128
D_FF = 2816
CONV_W = 3
LN_EPS = 1e-5
ALPHA = (2.0 * DEPTH) ** 0.25

LANES = 128
MXU_DIM = 256
VMEM_LIMIT = 56 * 1024 * 1024

HEAD_GROUP = MXU_DIM // NA_HEAD_DIM
N_HEAD_GROUPS = NA_HEADS // HEAD_GROUP
BAND = NA_WIN_ROWS * GRID_W
FF_CHUNK = MXU_DIM
N_FF_CHUNKS = D_FF // FF_CHUNK
HALO = 16
EPILOGUE_ROWS = 256
MASK_VALUE = -1e30
ATTN_UNROLL = 32


def _params(*sem):
    return pltpu.CompilerParams(dimension_semantics=sem, vmem_limit_bytes=VMEM_LIMIT)


def _layer_norm(h, g, b):
    mu = jnp.mean(h, axis=-1, keepdims=True)
    d = h - mu
    var = jnp.mean(d * d, axis=-1, keepdims=True)
    return d * lax.rsqrt(var + LN_EPS) * g + b


def _mod_kernel(c_ref, w_ref, b_ref, o_ref):
    c = c_ref[...]
    cs = c * jax.nn.sigmoid(c)
    o_ref[0] = jnp.dot(cs, w_ref[0], preferred_element_type=F32,
                       precision=lax.Precision.HIGHEST) + b_ref[0]


def _modulation(c, ada_w, ada_b):
    depth, d, n = ada_w.shape
    bsz = c.shape[0]
    tn = 1536
    return pl.pallas_call(
        _mod_kernel,
        grid=(depth, n // tn),
        in_specs=[pl.BlockSpec((bsz, d), lambda i, j: (0, 0)),
                  pl.BlockSpec((1, d, tn), lambda i, j: (i, 0, j)),
                  pl.BlockSpec((1, 1, tn), lambda i, j: (i, 0, j))],
        out_specs=pl.BlockSpec((1, bsz, tn), lambda i, j: (i, 0, j)),
        out_shape=jax.ShapeDtypeStruct((depth, bsz, n), F32),
        compiler_params=_params("parallel", "parallel"),
        name="adaln_mod",
    )(c, ada_w, ada_b.reshape(depth, 1, n))


def _mod_spec(piece):
    return pl.BlockSpec((1, 1, D_MODEL), lambda b, i: (b, 0, piece))


def _qkv_kernel(x_ref, sh_ref, sc_ref, w_ref, o_ref):
    u = (x_ref[0] * (1.0 + sc_ref[0]) + sh_ref[0]).astype(BF16)
    for j in range(3):
        cols = slice(j * D_MODEL, (j + 1) * D_MODEL)
        acc = jnp.dot(u, w_ref[:, cols], preferred_element_type=F32)
        if j == 0:
            acc = acc * (NA_HEAD_DIM ** -0.5)
        o_ref[0, :, cols] = acc.astype(BF16)


def _qkv_proj(x, mod, w_qkv, tm=512):
    bsz, s, d = x.shape
    return pl.pallas_call(
        _qkv_kernel,
        grid=(bsz, s // tm),
        in_specs=[pl.BlockSpec((1, tm, d), lambda b, i: (b, i, 0)),
                  _mod_spec(0), _mod_spec(1),
                  pl.BlockSpec((d, 3 * d), lambda b, i: (0, 0))],
        out_specs=pl.BlockSpec((1, tm, 3 * d), lambda b, i: (b, i, 0)),
        out_shape=jax.ShapeDtypeStruct((bsz, s, 3 * d), BF16),
        compiler_params=_params("parallel", "parallel"),
        name="qkv_proj",
    )(x, mod, mod, w_qkv)


def _bias_kernel(rpb_ref, o_ref):
    h = pl.program_id(0)
    n_dr = 2 * NA_WIN_ROWS - 1
    n_dc = 2 * NA_WIN_COLS - 1
    q = lax.broadcasted_iota(jnp.int32, (GRID_W, GRID_W), 0)
    k = lax.broadcasted_iota(jnp.int32, (GRID_W, GRID_W), 1)
    col_start = jnp.clip(q - NA_WIN_COLS // 2, 0, GRID_W - NA_WIN_COLS)
    col_in = (k >= col_start) & (k < col_start + NA_WIN_COLS)
    dc = jnp.clip(k - q + NA_WIN_COLS - 1, 0, n_dc - 1)
    tiles = []
    for dr in range(n_dr):
        t = jnp.full((GRID_W, GRID_W), MASK_VALUE, F32)
        for d in range(n_dc):
            val = rpb_ref[(h * n_dr + dr) * n_dc + d]
            t = jnp.where(col_in & (dc == d), val, t)
        tiles.append(t)
    for o in range(NA_WIN_ROWS):
        o_ref[o, 0] = jnp.concatenate(tiles[o:o + NA_WIN_ROWS], axis=1)


def _bias_table(rpb):
    bands = pl.pallas_call(
        _bias_kernel,
        grid=(NA_HEADS,),
        in_specs=[pl.BlockSpec(memory_space=pltpu.SMEM)],
        out_specs=pl.BlockSpec((NA_WIN_ROWS, 1, GRID_W, BAND), lambda h: (0, h, 0, 0)),
        out_shape=jax.ShapeDtypeStruct((NA_WIN_ROWS, NA_HEADS, GRID_W, BAND), F32),
        compiler_params=_params("parallel"),
        name="rpb_table",
    )(rpb.reshape(-1))
    return bands.reshape(NA_WIN_ROWS, NA_HEADS * GRID_W, BAND)


def _attn_kernel(q_ref, k_ref, v_ref, bias_ref, o_ref, *, rows):
    lane_head = lax.broadcasted_iota(jnp.int32, (GRID_W, MXU_DIM), 1) // NA_HEAD_DIM

    def one_row(r, carry):
        rs = jnp.clip(r - NA_WIN_ROWS // 2, 0, rows - NA_WIN_ROWS)
        off = rs - r + NA_WIN_ROWS - 1
        qr = q_ref[0, pl.ds(pl.multiple_of(r * GRID_W, GRID_W), GRID_W), :]
        zero = jnp.zeros_like(qr)
        q_heads = jnp.concatenate([jnp.where(lane_head == h, qr, zero) for h in range(HEAD_GROUP)], axis=0)
        band = pl.ds(pl.multiple_of(rs * GRID_W, GRID_W), BAND)
        kb = k_ref[0, band, :]
        s = lax.dot_general(q_heads, kb, (((1,), (1,)), ((), ())), preferred_element_type=F32)
        s = s + bias_ref[off]
        m = jnp.max(s, axis=-1, keepdims=True)
        p = jnp.exp(s - m)
        l = jnp.sum(p, axis=-1, keepdims=True)
        vb = v_ref[0, band, :]
        g = jnp.dot(p.astype(BF16), vb, preferred_element_type=F32) / l
        out = jnp.zeros((GRID_W, MXU_DIM), F32)
        for h in range(HEAD_GROUP):
            out = jnp.where(lane_head == h, g[h * GRID_W:(h + 1) * GRID_W, :], out)
        o_ref[0, pl.ds(pl.multiple_of(r * GRID_W, GRID_W), GRID_W), :] = out.astype(BF16)
        return carry

    lax.fori_loop(0, rows, one_row, 0, unroll=ATTN_UNROLL)


def _attention(qkv, bias):
    bsz, s, _ = qkv.shape
    rows = s // GRID_W
    assert rows >= NA_WIN_ROWS
    blk = lambda part: pl.BlockSpec((1, s, MXU_DIM), lambda b, g: (b, 0, part * N_HEAD_GROUPS + g))
    return pl.pallas_call(
        functools.partial(_attn_kernel, rows=rows),
        grid=(bsz, N_HEAD_GROUPS),
        in_specs=[blk(0), blk(1), blk(2),
                  pl.BlockSpec((NA_WIN_ROWS, HEAD_GROUP * GRID_W, BAND), lambda b, g: (0, g, 0))],
        out_specs=pl.BlockSpec((1, s, MXU_DIM), lambda b, g: (b, 0, g)),
        out_shape=jax.ShapeDtypeStruct((bsz, s, D_MODEL), BF16),
        compiler_params=_params("parallel", "parallel"),
        name="nbr_attention",
    )(qkv, qkv, qkv, bias)


def _proj_ln_kernel(y_ref, w_ref, x_ref, gate_ref, g_ref, b_ref, o_ref):
    for j in range(y_ref.shape[1] // EPILOGUE_ROWS):
        rows = slice(j * EPILOGUE_ROWS, (j + 1) * EPILOGUE_ROWS)
        y = jnp.dot(y_ref[0, rows, :], w_ref[...], preferred_element_type=F32)
        h = ALPHA * x_ref[0, rows, :] + (1.0 + gate_ref[0]) * y
        o_ref[0, rows, :] = _layer_norm(h, g_ref[...], b_ref[...])


def _proj_ln(y, w, x, mod, ln_g, ln_b, tm=1024):
    bsz, s, d = x.shape
    vec = pl.BlockSpec((1, d), lambda b, i: (0, 0))
    return pl.pallas_call(
        _proj_ln_kernel,
        grid=(bsz, s // tm),
        in_specs=[pl.BlockSpec((1, tm, d), lambda b, i: (b, i, 0)),
                  pl.BlockSpec((d, d), lambda b, i: (0, 0)),
                  pl.BlockSpec((1, tm, d), lambda b, i: (b, i, 0)),
                  _mod_spec(2), vec, vec],
        out_specs=pl.BlockSpec((1, tm, d), lambda b, i: (b, i, 0)),
        out_shape=jax.ShapeDtypeStruct((bsz, s, d), F32),
        compiler_params=_params("parallel", "parallel"),
        name="proj_ln",
    )(y, w, x, mod, ln_g.reshape(1, d), ln_b.reshape(1, d))


def _gelu_tanh(x):
    c = np.float32(np.sqrt(2.0 / np.pi))
    return x * (0.5 * (1.0 + jnp.tanh(c * (x + 0.044715 * (x * x * x)))))


def _ffn_kernel(x_ref, xp_ref, xn_ref, sh_ref, sc_ref, gate_ref, wup_ref, cw_ref, cb_ref, wd_ref,
                g_ref, b_ref, o_ref, u_ref, *, tm):
    i = pl.program_id(1)
    scale = 1.0 + sc_ref[0]
    shift = sh_ref[0]
    x = x_ref[0]
    prev = jnp.where(i > 0, xp_ref[0] * scale + shift, 0.0)
    nxt = jnp.where(i < pl.num_programs(1) - 1, xn_ref[0] * scale + shift, 0.0)
    u_ref[0:HALO, :] = prev.astype(BF16)
    u_ref[HALO:HALO + tm, :] = (x * scale + shift).astype(BF16)
    u_ref[HALO + tm:, :] = nxt.astype(BF16)
    ext = tm + 2 * HALO
    cols = lambda c: slice(c * FF_CHUNK, (c + 1) * FF_CHUNK)

    def up_proj(c):
        a = jnp.dot(u_ref[...], wup_ref[:, cols(c)], preferred_element_type=F32)
        gt = jnp.dot(u_ref[HALO:HALO + tm, :], wup_ref[:, cols(N_FF_CHUNKS + c)], preferred_element_type=F32)
        return a, gt

    a, gt = up_proj(0)
    hidden = []
    for c in range(N_FF_CHUNKS):
        ahead = up_proj(c + 1) if c + 1 < N_FF_CHUNKS else None
        cw = cw_ref[:, cols(c)]
        a_prev = pltpu.roll(a, 1, 0)[HALO:HALO + tm]
        a_next = pltpu.roll(a, ext - 1, 0)[HALO:HALO + tm]
        conv = cb_ref[:, cols(c)] + a_prev * cw[0:1] + a[HALO:HALO + tm] * cw[1:2] + a_next * cw[2:3]
        hidden.append((_gelu_tanh(conv) * gt).astype(BF16))
        if ahead is not None:
            a, gt = ahead

    hid = jnp.concatenate(hidden, axis=1)
    for j in range(tm // EPILOGUE_ROWS):
        rows = slice(j * EPILOGUE_ROWS, (j + 1) * EPILOGUE_ROWS)
        y = jnp.dot(hid[rows], wd_ref[...], preferred_element_type=F32)
        h = ALPHA * x[rows] + (1.0 + gate_ref[0]) * y
        o_ref[0, rows, :] = _layer_norm(h, g_ref[...], b_ref[...])


def _conv_ffn(x, mod, w_up, conv_w, conv_b, w_down, ln_g, ln_b, tm=512):
    bsz, s, d = x.shape
    nblk = s // tm
    hb = tm // HALO
    n_halo = s // HALO
    vec = pl.BlockSpec((1, d), lambda b, i: (0, 0))
    full = lambda a: pl.BlockSpec(a.shape, lambda b, i: (0, 0), pipeline_mode=pl.Buffered(1))
    wup = w_up.astype(BF16)
    wd = w_down.astype(BF16)
    cb = conv_b.reshape(1, D_FF)
    return pl.pallas_call(
        functools.partial(_ffn_kernel, tm=tm),
        grid=(bsz, nblk),
        in_specs=[pl.BlockSpec((1, tm, d), lambda b, i: (b, i, 0)),
                  pl.BlockSpec((1, HALO, d), lambda b, i: (b, jnp.maximum(i * hb - 1, 0), 0)),
                  pl.BlockSpec((1, HALO, d), lambda b, i: (b, jnp.minimum((i + 1) * hb, n_halo - 1), 0)),
                  _mod_spec(3), _mod_spec(4), _mod_spec(5),
                  full(wup), full(conv_w), full(cb), full(wd), vec, vec],
        out_specs=pl.BlockSpec((1, tm, d), lambda b, i: (b, i, 0)),
        out_shape=jax.ShapeDtypeStruct((bsz, s, d), F32),
        scratch_shapes=[pltpu.VMEM((tm + 2 * HALO, d), BF16)],
        compiler_params=_params("parallel", "arbitrary"),
        name="conv_ffn",
    )(x, x, x, mod, mod, mod, wup, conv_w, cb, wd, ln_g.reshape(1, d), ln_b.reshape(1, d))


FFT_N1 = 64
FFT_N2 = 64
FFT_ROWS = 512
FFT_UNROLL = 16
FFT_PITCH = 72


def _fourier_consts():
    s = FFT_N1 * FFT_N2
    c = np.arange(FN_GROUP_DIM)
    ang = 2.0 * np.pi * np.outer(c, c) / FN_GROUP_DIM
    blk = lambda m: np.kron(np.eye(MXU_DIM // FN_GROUP_DIM), m)
    scale = 1.0 / np.sqrt(s * FN_GROUP_DIM)
    chan = np.concatenate([blk(np.cos(ang)), -blk(np.sin(ang))], axis=1) * scale
    k1 = np.arange(FFT_N1)[None, :, None]
    a = np.arange(FFT_N1)[None, None, :]
    b = np.arange(FFT_N2)[:, None, None]
    th = 2.0 * np.pi * ((k1 * (FFT_N2 * a + b)) % s) / s
    st1 = np.concatenate([np.concatenate([np.cos(th), np.sin(th)], axis=2),
                          np.concatenate([-np.sin(th), np.cos(th)], axis=2)], axis=1)
    k2 = np.arange(FFT_N2)[:, None]
    bb = np.arange(FFT_N2)[None, :]
    ph = 2.0 * np.pi * ((k2 * bb) % FFT_N2) / FFT_N2
    st2 = np.concatenate([np.cos(ph), np.sin(ph)], axis=1)
    f32 = lambda m: jnp.asarray(m.astype(np.float32))
    return f32(chan).astype(BF16), f32(st1).astype(BF16), f32(st2).astype(BF16)


def _fourier_kernel(x_ref, sh_ref, sc_ref, chan_ref, st1_ref, st2_ref, o_ref, z_ref, y_ref):
    s = FFT_N1 * FFT_N2
    n_parts = 2 * MXU_DIM // LANES
    groups = FFT_ROWS // FFT_N2
    scale = 1.0 + sc_ref[0]
    shift = sh_ref[0]
    group_rows = lambda g: pl.ds(pl.multiple_of(g * FFT_PITCH, 8), FFT_N2)

    def chan_dft(j, carry):
        rows = pl.ds(pl.multiple_of(j * FFT_ROWS, FFT_ROWS), FFT_ROWS)
        u = (x_ref[0, rows, :] * scale + shift).astype(BF16)
        z = jnp.dot(u, chan_ref[...], preferred_element_type=F32)
        for i in range(groups):
            for p in range(n_parts):
                z_ref[p, group_rows(j * groups + i), :] = z[i * FFT_N2:(i + 1) * FFT_N2, p * LANES:(p + 1) * LANES]
        return carry

    lax.fori_loop(0, s // FFT_ROWS, chan_dft, 0, unroll=2)

    def gather(ref, r):
        take = pl.ds(r, FFT_N1, stride=FFT_PITCH)
        re = jnp.concatenate([ref[0, take, :], ref[1, take, :]], axis=1)
        im = jnp.concatenate([ref[2, take, :], ref[3, take, :]], axis=1)
        return jnp.concatenate([re, im], axis=0).astype(BF16)

    def stage1(b, carry):
        y = jnp.dot(st1_ref[b], gather(z_ref, b), preferred_element_type=F32)
        for p in range(n_parts):
            half, lane = divmod(p, 2)
            y_ref[p, group_rows(b), :] = y[half * FFT_N1:(half + 1) * FFT_N1, lane * LANES:(lane + 1) * LANES]
        return carry

    lax.fori_loop(0, FFT_N2, stage1, 0, unroll=FFT_UNROLL)

    def stage2(k1, carry):
        out = jnp.dot(st2_ref[...], gather(y_ref, k1), preferred_element_type=F32)
        for lane in range(MXU_DIM // LANES):
            z_ref[lane, pl.ds(k1, FFT_N2, stride=FFT_PITCH), :] = out[:, lane * LANES:(lane + 1) * LANES]
        return carry

    lax.fori_loop(0, FFT_N1, stage2, 0, unroll=FFT_UNROLL)
    for k2 in range(FFT_N2):
        src = slice(k2 * FFT_PITCH, k2 * FFT_PITCH + FFT_N1)
        o_ref[0, k2 * FFT_N1:(k2 + 1) * FFT_N1, :] = jnp.concatenate(
            [z_ref[0, src, :], z_ref[1, src, :]], axis=1).astype(BF16)


def _fourier_mix(x, mod):
    bsz, s, d = x.shape
    assert s == FFT_N1 * FFT_N2
    chan, st1, st2 = _fourier_consts()
    mspec = lambda piece: pl.BlockSpec((1, 1, MXU_DIM), lambda b, g: (b, 0, piece * (d // MXU_DIM) + g))
    buf = pltpu.VMEM((2 * MXU_DIM // LANES, FFT_N1 * FFT_PITCH, LANES), F32)
    return pl.pallas_call(
        _fourier_kernel,
        grid=(bsz, d // MXU_DIM),
        in_specs=[pl.BlockSpec((1, s, MXU_DIM), lambda b, g: (b, 0, g)),
                  mspec(0), mspec(1),
                  pl.BlockSpec(chan.shape, lambda b, g: (0, 0)),
                  pl.BlockSpec(st1.shape, lambda b, g: (0, 0, 0)),
                  pl.BlockSpec(st2.shape, lambda b, g: (0, 0))],
        out_specs=pl.BlockSpec((1, s, MXU_DIM), lambda b, g: (b, 0, g)),
        out_shape=jax.ShapeDtypeStruct((bsz, s, d), BF16),
        scratch_shapes=[buf, buf],
        compiler_params=_params("parallel", "parallel"),
        name="fourier_mix",
    )(x, mod, mod, chan, st1, st2)


def kernel(x, c, ada_w, ada_b, na_w_qkv, na_rpb, na_w_o, fn_w_o, ln1_g, ln1_b,
           ffn_w_up, ffn_conv_w, ffn_conv_b, ffn_w_down, ln2_g, ln2_b):
    bsz = x.shape[0]
    mod_all = _modulation(c, ada_w, ada_b)
    for i in range(DEPTH):
        mod = mod_all[i].reshape(bsz, 1, 6 * D_MODEL)
        j = i // 2
        if i % 2 == 0:
            qkv = _qkv_proj(x, mod, na_w_qkv[j].astype(BF16))
            y = _attention(qkv, _bias_table(na_rpb[j]))
            w_o = na_w_o[j]
        else:
            y = _fourier_mix(x, mod)
            w_o = fn_w_o[j]
        x = _proj_ln(y, w_o.astype(BF16), x, mod, ln1_g[i], ln1_b[i])
        x = _conv_ffn(x, mod, ffn_w_up[i], ffn_conv_w[i], ffn_conv_b[i], ffn_w_down[i], ln2_g[i], ln2_b[i])
    return x
```

```python
import functools

import numpy as np
import jax
import jax.numpy as jnp
from jax import lax
from jax.experimental import pallas as pl
from jax.experimental.pallas import tpu as pltpu

F32 = jnp.float32
BF16 = jnp.bfloat16

D_MODEL = 1024
DEPTH = 2
GRID_W = 64
NA_HEADS = 16
NA_HEAD_DIM = D_MODEL // NA_HEADS
NA_WIN_ROWS = 8
NA_WIN_COLS = 16
FN_GROUP_DIM = 128
D_FF = 2816
CONV_W = 3
LN_EPS = 1e-5
ALPHA = (2.0 * DEPTH) ** 0.25

LANES = 128
MXU_DIM = 256
VMEM_LIMIT = 56 * 1024 * 1024

HEAD_GROUP = MXU_DIM // NA_HEAD_DIM
N_HEAD_GROUPS = NA_HEADS // HEAD_GROUP
BAND = NA_WIN_ROWS * GRID_W
FF_CHUNK = MXU_DIM
N_FF_CHUNKS = D_FF // FF_CHUNK
HALO = 16
EPILOGUE_ROWS = 256
MASK_VALUE = -1e30
ATTN_UNROLL = 32


def _params(*sem):
    return pltpu.CompilerParams(dimension_semantics=sem, vmem_limit_bytes=VMEM_LIMIT)


def _layer_norm(h, g, b):
    mu = jnp.mean(h, axis=-1, keepdims=True)
    d = h - mu
    var = jnp.mean(d * d, axis=-1, keepdims=True)
    return d * lax.rsqrt(var + LN_EPS) * g + b


def _mod_kernel(c_ref, w_ref, b_ref, o_ref):
    c = c_ref[...]
    cs = c * jax.nn.sigmoid(c)
    o_ref[0] = jnp.dot(cs, w_ref[0], preferred_element_type=F32,
                       precision=lax.Precision.HIGHEST) + b_ref[0]


def _modulation(c, ada_w, ada_b):
    depth, d, n = ada_w.shape
    bsz = c.shape[0]
    tn = 1536
    return pl.pallas_call(
        _mod_kernel,
        grid=(depth, n // tn),
        in_specs=[pl.BlockSpec((bsz, d), lambda i, j: (0, 0)),
                  pl.BlockSpec((1, d, tn), lambda i, j: (i, 0, j)),
                  pl.BlockSpec((1, 1, tn), lambda i, j: (i, 0, j))],
        out_specs=pl.BlockSpec((1, bsz, tn), lambda i, j: (i, 0, j)),
        out_shape=jax.ShapeDtypeStruct((depth, bsz, n), F32),
        compiler_params=_params("parallel", "parallel"),
        name="adaln_mod",
    )(c, ada_w, ada_b.reshape(depth, 1, n))


def _mod_spec(piece):
    return pl.BlockSpec((1, 1, D_MODEL), lambda b, i: (b, 0, piece))


def _qkv_kernel(x_ref, sh_ref, sc_ref, w_ref, o_ref):
    u = (x_ref[0] * (1.0 + sc_ref[0]) + sh_ref[0]).astype(BF16)
    for j in range(3):
        cols = slice(j * D_MODEL, (j + 1) * D_MODEL)
        acc = jnp.dot(u, w_ref[:, cols], preferred_element_type=F32)
        if j == 0:
            acc = acc * (NA_HEAD_DIM ** -0.5)
        o_ref[0, :, cols] = acc.astype(BF16)


def _qkv_proj(x, mod, w_qkv, tm=1024):
    bsz, s, d = x.shape
    return pl.pallas_call(
        _qkv_kernel,
        grid=(bsz, s // tm),
        in_specs=[pl.BlockSpec((1, tm, d), lambda b, i: (b, i, 0)),
                  _mod_spec(0), _mod_spec(1),
                  pl.BlockSpec((d, 3 * d), lambda b, i: (0, 0), pipeline_mode=pl.Buffered(1))],
        out_specs=pl.BlockSpec((1, tm, 3 * d), lambda b, i: (b, i, 0)),
        out_shape=jax.ShapeDtypeStruct((bsz, s, 3 * d), BF16),
        compiler_params=_params("parallel", "parallel"),
        name="qkv_proj",
    )(x, mod, mod, w_qkv)


def _bias_kernel(rpb_ref, o_ref):
    h = pl.program_id(0)
    n_dr = 2 * NA_WIN_ROWS - 1
    n_dc = 2 * NA_WIN_COLS - 1
    q = lax.broadcasted_iota(jnp.int32, (GRID_W, GRID_W), 0)
    k = lax.broadcasted_iota(jnp.int32, (GRID_W, GRID_W), 1)
    col_start = jnp.clip(q - NA_WIN_COLS // 2, 0, GRID_W - NA_WIN_COLS)
    col_in = (k >= col_start) & (k < col_start + NA_WIN_COLS)
    dc = jnp.clip(k - q + NA_WIN_COLS - 1, 0, n_dc - 1)
    tiles = []
    for dr in range(n_dr):
        t = jnp.full((GRID_W, GRID_W), MASK_VALUE, F32)
        for d in range(n_dc):
            val = rpb_ref[(h * n_dr + dr) * n_dc + d]
            t = jnp.where(col_in & (dc == d), val, t)
        tiles.append(t)
    for o in range(NA_WIN_ROWS):
        o_ref[o, 0] = jnp.concatenate(tiles[o:o + NA_WIN_ROWS], axis=1)


def _bias_table(rpb):
    bands = pl.pallas_call(
        _bias_kernel,
        grid=(NA_HEADS,),
        in_specs=[pl.BlockSpec(memory_space=pltpu.SMEM)],
        out_specs=pl.BlockSpec((NA_WIN_ROWS, 1, GRID_W, BAND), lambda h: (0, h, 0, 0)),
        out_shape=jax.ShapeDtypeStruct((NA_WIN_ROWS, NA_HEADS, GRID_W, BAND), F32),
        compiler_params=_params("parallel"),
        name="rpb_table",
    )(rpb.reshape(-1))
    return bands.reshape(NA_WIN_ROWS, NA_HEADS * GRID_W, BAND)


def _attn_kernel(q_ref, k_ref, v_ref, bias_ref, o_ref, *, rows):
    lane_head = lax.broadcasted_iota(jnp.int32, (GRID_W, MXU_DIM), 1) // NA_HEAD_DIM

    def one_row(r, carry):
        rs = jnp.clip(r - NA_WIN_ROWS // 2, 0, rows - NA_WIN_ROWS)
        off = rs - r + NA_WIN_ROWS - 1
        qr = q_ref[0, pl.ds(pl.multiple_of(r * GRID_W, GRID_W), GRID_W), :]
        zero = jnp.zeros_like(qr)
        q_heads = jnp.concatenate([jnp.where(lane_head == h, qr, zero) for h in range(HEAD_GROUP)], axis=0)
        band = pl.ds(pl.multiple_of(rs * GRID_W, GRID_W), BAND)
        kb = k_ref[0, band, :]
        s = lax.dot_general(q_heads, kb, (((1,), (1,)), ((), ())), preferred_element_type=F32)
        s = s + bias_ref[off]
        m = jnp.max(s, axis=-1, keepdims=True)
        p = jnp.exp(s - m)
        l = jnp.sum(p, axis=-1, keepdims=True)
        vb = v_ref[0, band, :]
        g = jnp.dot(p.astype(BF16), vb, preferred_element_type=F32) / l
        out = jnp.zeros((GRID_W, MXU_DIM), F32)
        for h in range(HEAD_GROUP):
            out = jnp.where(lane_head == h, g[h * GRID_W:(h + 1) * GRID_W, :], out)
        o_ref[0, pl.ds(pl.multiple_of(r * GRID_W, GRID_W), GRID_W), :] = out.astype(BF16)
        return carry

    lax.fori_loop(0, rows, one_row, 0, unroll=ATTN_UNROLL)


def _attention(qkv, bias):
    bsz, s, _ = qkv.shape
    rows = s // GRID_W
    assert rows >= NA_WIN_ROWS
    blk = lambda part: pl.BlockSpec((1, s, MXU_DIM), lambda b, g: (b, 0, part * N_HEAD_GROUPS + g))
    return pl.pallas_call(
        functools.partial(_attn_kernel, rows=rows),
        grid=(bsz, N_HEAD_GROUPS),
        in_specs=[blk(0), blk(1), blk(2),
                  pl.BlockSpec((NA_WIN_ROWS, HEAD_GROUP * GRID_W, BAND), lambda b, g: (0, g, 0))],
        out_specs=pl.BlockSpec((1, s, MXU_DIM), lambda b, g: (b, 0, g)),
        out_shape=jax.ShapeDtypeStruct((bsz, s, D_MODEL), BF16),
        compiler_params=_params("parallel", "parallel"),
        name="nbr_attention",
    )(qkv, qkv, qkv, bias)


def _proj_ln_kernel(y_ref, w_ref, x_ref, gate_ref, g_ref, b_ref, o_ref):
    for j in range(y_ref.shape[1] // EPILOGUE_ROWS):
        rows = slice(j * EPILOGUE_ROWS, (j + 1) * EPILOGUE_ROWS)
        y = jnp.dot(y_ref[0, rows, :], w_ref[...], preferred_element_type=F32)
        h = ALPHA * x_ref[0, rows, :] + (1.0 + gate_ref[0]) * y
        o_ref[0, rows, :] = _layer_norm(h, g_ref[...], b_ref[...])


def _proj_ln(y, w, x, mod, ln_g, ln_b, tm=1024):
    bsz, s, d = x.shape
    vec = pl.BlockSpec((1, d), lambda b, i: (0, 0))
    return pl.pallas_call(
        _proj_ln_kernel,
        grid=(bsz, s // tm),
        in_specs=[pl.BlockSpec((1, tm, d), lambda b, i: (b, i, 0)),
                  pl.BlockSpec((d, d), lambda b, i: (0, 0)),
                  pl.BlockSpec((1, tm, d), lambda b, i: (b, i, 0)),
                  _mod_spec(2), vec, vec],
        out_specs=pl.BlockSpec((1, tm, d), lambda b, i: (b, i, 0)),
        out_shape=jax.ShapeDtypeStruct((bsz, s, d), F32),
        compiler_params=_params("parallel", "parallel"),
        name="proj_ln",
    )(y, w, x, mod, ln_g.reshape(1, d), ln_b.reshape(1, d))


def _gelu_tanh(x):
    c = np.float32(np.sqrt(2.0 / np.pi))
    return x * (0.5 * (1.0 + jnp.tanh(c * (x + 0.044715 * (x * x * x)))))


def _ffn_kernel(x_ref, xp_ref, xn_ref, sh_ref, sc_ref, gate_ref, wup_ref, cw_ref, cb_ref, wd_ref,
                g_ref, b_ref, o_ref, u_ref, *, tm):
    i = pl.program_id(1)
    scale = 1.0 + sc_ref[0]
    shift = sh_ref[0]
    x = x_ref[0]
    prev = jnp.where(i > 0, xp_ref[0] * scale + shift, 0.0)
    nxt = jnp.where(i < pl.num_programs(1) - 1, xn_ref[0] * scale + shift, 0.0)
    u_ref[0:HALO, :] = prev.astype(BF16)
    u_ref[HALO:HALO + tm, :] = (x * scale + shift).astype(BF16)
    u_ref[HALO + tm:, :] = nxt.astype(BF16)
    ext = tm + 2 * HALO
    cols = lambda c: slice(c * FF_CHUNK, (c + 1) * FF_CHUNK)

    def up_proj(c):
        a = jnp.dot(u_ref[...], wup_ref[:, cols(c)], preferred_element_type=F32)
        gt = jnp.dot(u_ref[HALO:HALO + tm, :], wup_ref[:, cols(N_FF_CHUNKS + c)], preferred_element_type=F32)
        return a, gt

    a, gt = up_proj(0)
    hidden = []
    for c in range(N_FF_CHUNKS):
        ahead = up_proj(c + 1) if c + 1 < N_FF_CHUNKS else None
        cw = cw_ref[:, cols(c)]
        a_prev = pltpu.roll(a, 1, 0)[HALO:HALO + tm]
        a_next = pltpu.roll(a, ext - 1, 0)[HALO:HALO + tm]
        conv = cb_ref[:, cols(c)] + a_prev * cw[0:1] + a[HALO:HALO + tm] * cw[1:2] + a_next * cw[2:3]
        hidden.append((_gelu_tanh(conv) * gt).astype(BF16))
        if ahead is not None:
            a, gt = ahead

    hid = jnp.concatenate(hidden, axis=1)
    for j in range(tm // EPILOGUE_ROWS):
        rows = slice(j * EPILOGUE_ROWS, (j + 1) * EPILOGUE_ROWS)
        y = jnp.dot(hid[rows], wd_ref[...], preferred_element_type=F32)
        h = ALPHA * x[rows] + (1.0 + gate_ref[0]) * y
        o_ref[0, rows, :] = _layer_norm(h, g_ref[...], b_ref[...])


def _conv_ffn(x, mod, w_up, conv_w, conv_b, w_down, ln_g, ln_b, tm=1024):
    bsz, s, d = x.shape
    nblk = s // tm
    hb = tm // HALO
    n_halo = s // HALO
    vec = pl.BlockSpec((1, d), lambda b, i: (0, 0))
    full = lambda a: pl.BlockSpec(a.shape, lambda b, i: (0, 0), pipeline_mode=pl.Buffered(1))
    wup = w_up.astype(BF16)
    wd = w_down.astype(BF16)
    cb = conv_b.reshape(1, D_FF)
    return pl.pallas_call(
        functools.partial(_ffn_kernel, tm=tm),
        grid=(bsz, nblk),
        in_specs=[pl.BlockSpec((1, tm, d), lambda b, i: (b, i, 0)),
                  pl.BlockSpec((1, HALO, d), lambda b, i: (b, jnp.maximum(i * hb - 1, 0), 0)),
                  pl.BlockSpec((1, HALO, d), lambda b, i: (b, jnp.minimum((i + 1) * hb, n_halo - 1), 0)),
                  _mod_spec(3), _mod_spec(4), _mod_spec(5),
                  full(wup), full(conv_w), full(cb), full(wd), vec, vec],
        out_specs=pl.BlockSpec((1, tm, d), lambda b, i: (b, i, 0)),
        out_shape=jax.ShapeDtypeStruct((bsz, s, d), F32),
        scratch_shapes=[pltpu.VMEM((tm + 2 * HALO, d), BF16)],
        compiler_params=_params("parallel", "arbitrary"),
        name="conv_ffn",
    )(x, x, x, mod, mod, mod, wup, conv_w, cb, wd, ln_g.reshape(1, d), ln_b.reshape(1, d))


FFT_N1 = 64
FFT_N2 = 64
FFT_ROWS = 512
FFT_UNROLL = 16
FFT_PITCH = 72


def _fourier_consts():
    s = FFT_N1 * FFT_N2
    c = np.arange(FN_GROUP_DIM)
    ang = 2.0 * np.pi * np.outer(c, c) / FN_GROUP_DIM
    blk = lambda m: np.kron(np.eye(MXU_DIM // FN_GROUP_DIM), m)
    scale = 1.0 / np.sqrt(s * FN_GROUP_DIM)
    chan = np.concatenate([blk(np.cos(ang)), -blk(np.sin(ang))], axis=1) * scale
    k1 = np.arange(FFT_N1)[None, :, None]
    a = np.arange(FFT_N1)[None, None, :]
    b = np.arange(FFT_N2)[:, None, None]
    th = 2.0 * np.pi * ((k1 * (FFT_N2 * a + b)) % s) / s
    st1 = np.concatenate([np.concatenate([np.cos(th), np.sin(th)], axis=2),
                          np.concatenate([-np.sin(th), np.cos(th)], axis=2)], axis=1)
    k2 = np.arange(FFT_N2)[:, None]
    bb = np.arange(FFT_N2)[None, :]
    ph = 2.0 * np.pi * ((k2 * bb) % FFT_N2) / FFT_N2
    st2 = np.concatenate([np.cos(ph), np.sin(ph)], axis=1)
    f32 = lambda m: jnp.asarray(m.astype(np.float32))
    return f32(chan).astype(BF16), f32(st1).astype(BF16), f32(st2).astype(BF16)


def _fourier_kernel(x_ref, sh_ref, sc_ref, chan_ref, st1_ref, st2_ref, o_ref, z_ref, y_ref):
    s = FFT_N1 * FFT_N2
    n_parts = 2 * MXU_DIM // LANES
    groups = FFT_ROWS // FFT_N2
    scale = 1.0 + sc_ref[0]
    shift = sh_ref[0]
    group_rows = lambda g: pl.ds(pl.multiple_of(g * FFT_PITCH, 8), FFT_N2)

    def chan_dft(j, carry):
        rows = pl.ds(pl.multiple_of(j * FFT_ROWS, FFT_ROWS), FFT_ROWS)
        u = (x_ref[0, rows, :] * scale + shift).astype(BF16)
        z = jnp.dot(u, chan_ref[...], preferred_element_type=F32)
        for i in range(groups):
            for p in range(n_parts):
                z_ref[p, group_rows(j * groups + i), :] = z[i * FFT_N2:(i + 1) * FFT_N2, p * LANES:(p + 1) * LANES]
        return carry

    lax.fori_loop(0, s // FFT_ROWS, chan_dft, 0, unroll=2)

    def gather(ref, r):
        take = pl.ds(r, FFT_N1, stride=FFT_PITCH)
        re = jnp.concatenate([ref[0, take, :], ref[1, take, :]], axis=1)
        im = jnp.concatenate([ref[2, take, :], ref[3, take, :]], axis=1)
        return jnp.concatenate([re, im], axis=0).astype(BF16)

    def stage1(b, carry):
        y = jnp.dot(st1_ref[b], gather(z_ref, b), preferred_element_type=F32)
        for p in range(n_parts):
            half, lane = divmod(p, 2)
            y_ref[p, group_rows(b), :] = y[half * FFT_N1:(half + 1) * FFT_N1, lane * LANES:(lane + 1) * LANES]
        return carry

    lax.fori_loop(0, FFT_N2, stage1, 0, unroll=FFT_UNROLL)

    def stage2(k1, carry):
        out = jnp.dot(st2_ref[...], gather(y_ref, k1), preferred_element_type=F32)
        for lane in range(MXU_DIM // LANES):
            z_ref[lane, pl.ds(k1, FFT_N2, stride=FFT_PITCH), :] = out[:, lane * LANES:(lane + 1) * LANES]
        return carry

    lax.fori_loop(0, FFT_N1, stage2, 0, unroll=FFT_UNROLL)
    for k2 in range(FFT_N2):
        src = slice(k2 * FFT_PITCH, k2 * FFT_PITCH + FFT_N1)
        o_ref[0, k2 * FFT_N1:(k2 + 1) * FFT_N1, :] = jnp.concatenate(
            [z_ref[0, src, :], z_ref[1, src, :]], axis=1).astype(BF16)


def _fourier_mix(x, mod):
    bsz, s, d = x.shape
    assert s == FFT_N1 * FFT_N2
    chan, st1, st2 = _fourier_consts()
    mspec = lambda piece: pl.BlockSpec((1, 1, MXU_DIM), lambda b, g: (b, 0, piece * (d // MXU_DIM) + g))
    buf = pltpu.VMEM((2 * MXU_DIM // LANES, FFT_N1 * FFT_PITCH, LANES), F32)
    return pl.pallas_call(
        _fourier_kernel,
        grid=(bsz, d // MXU_DIM),
        in_specs=[pl.BlockSpec((1, s, MXU_DIM), lambda b, g: (b, 0, g)),
                  mspec(0), mspec(1),
                  pl.BlockSpec(chan.shape, lambda b, g: (0, 0)),
                  pl.BlockSpec(st1.shape, lambda b, g: (0, 0, 0)),
                  pl.BlockSpec(st2.shape, lambda b, g: (0, 0))],
        out_specs=pl.BlockSpec((1, s, MXU_DIM), lambda b, g: (b, 0, g)),
        out_shape=jax.ShapeDtypeStruct((bsz, s, d), BF16),
        scratch_shapes=[buf, buf],
        compiler_params=_params("parallel", "parallel"),
        name="fourier_mix",
    )(x, mod, mod, chan, st1, st2)


def kernel(x, c, ada_w, ada_b, na_w_qkv, na_rpb, na_w_o, fn_w_o, ln1_g, ln1_b,
           ffn_w_up, ffn_conv_w, ffn_conv_b, ffn_w_down, ln2_g, ln2_b):
    bsz = x.shape[0]
    mod_all = _modulation(c, ada_w, ada_b)
    for i in range(DEPTH):
        mod = mod_all[i].reshape(bsz, 1, 6 * D_MODEL)
        j = i // 2
        if i % 2 == 0:
            qkv = _qkv_proj(x, mod, na_w_qkv[j].astype(BF16))
            y = _attention(qkv, _bias_table(na_rpb[j]))
            w_o = na_w_o[j]
        else:
            y = _fourier_mix(x, mod)
            w_o = fn_w_o[j]
        x = _proj_ln(y, w_o.astype(BF16), x, mod, ln1_g[i], ln1_b[i])
        x = _conv_ffn(x, mod, ffn_w_up[i], ffn_conv_w[i], ffn_conv_b[i], ffn_w_down[i], ln2_g[i], ln2_b[i])
    return x
```

```python
import functools

import numpy as np
import jax
import jax.numpy as jnp
from jax import lax
from jax.experimental import pallas as pl
from jax.experimental.pallas import tpu as pltpu

F32 = jnp.float32
BF16 = jnp.bfloat16

D_MODEL = 1024
DEPTH = 2
GRID_W = 64
NA_HEADS = 16
NA_HEAD_DIM = D_MODEL // NA_HEADS
NA_WIN_ROWS = 8
NA_WIN_COLS = 16
FN_GROUP_DIM = 128
D_FF = 2816
CONV_W = 3
LN_EPS = 1e-5
ALPHA = (2.0 * DEPTH) ** 0.25

LANES = 128
MXU_DIM = 256
VMEM_LIMIT = 56 * 1024 * 1024

HEAD_GROUP = MXU_DIM // NA_HEAD_DIM
N_HEAD_GROUPS = NA_HEADS // HEAD_GROUP
BAND = NA_WIN_ROWS * GRID_W
FF_CHUNK = MXU_DIM
N_FF_CHUNKS = D_FF // FF_CHUNK
HALO = 16
EPILOGUE_ROWS = 256
MASK_VALUE = -1e30
ATTN_UNROLL = 32


def _params(*sem):
    return pltpu.CompilerParams(dimension_semantics=sem, vmem_limit_bytes=VMEM_LIMIT)


def _layer_norm(h, g, b):
    mu = jnp.mean(h, axis=-1, keepdims=True)
    d = h - mu
    var = jnp.mean(d * d, axis=-1, keepdims=True)
    return d * lax.rsqrt(var + LN_EPS) * g + b


def _mod_kernel(c_ref, w_ref, b_ref, o_ref):
    c = c_ref[...]
    cs = c * jax.nn.sigmoid(c)
    o_ref[0] = jnp.dot(cs, w_ref[0], preferred_element_type=F32,
                       precision=lax.Precision.HIGHEST) + b_ref[0]


def _modulation(c, ada_w, ada_b):
    depth, d, n = ada_w.shape
    bsz = c.shape[0]
    tn = 1536
    return pl.pallas_call(
        _mod_kernel,
        grid=(depth, n // tn),
        in_specs=[pl.BlockSpec((bsz, d), lambda i, j: (0, 0)),
                  pl.BlockSpec((1, d, tn), lambda i, j: (i, 0, j)),
                  pl.BlockSpec((1, 1, tn), lambda i, j: (i, 0, j))],
        out_specs=pl.BlockSpec((1, bsz, tn), lambda i, j: (i, 0, j)),
        out_shape=jax.ShapeDtypeStruct((depth, bsz, n), F32),
        compiler_params=_params("parallel", "parallel"),
        name="adaln_mod",
    )(c, ada_w, ada_b.reshape(depth, 1, n))


def _mod_spec(piece):
    return pl.BlockSpec((1, 1, D_MODEL), lambda b, i: (b, 0, piece))


def _qkv_kernel(x_ref, sh_ref, sc_ref, w_ref, o_ref):
    u = (x_ref[0] * (1.0 + sc_ref[0]) + sh_ref[0]).astype(BF16)
    for j in range(3):
        cols = slice(j * D_MODEL, (j + 1) * D_MODEL)
        acc = jnp.dot(u, w_ref[:, cols], preferred_element_type=F32)
        if j == 0:
            acc = acc * (NA_HEAD_DIM ** -0.5)
        o_ref[0, :, cols] = acc.astype(BF16)


def _qkv_proj(x, mod, w_qkv, tm=1024):
    bsz, s, d = x.shape
    return pl.pallas_call(
        _qkv_kernel,
        grid=(bsz, s // tm),
        in_specs=[pl.BlockSpec((1, tm, d), lambda b, i: (b, i, 0)),
                  _mod_spec(0), _mod_spec(1),
                  pl.BlockSpec((d, 3 * d), lambda b, i: (0, 0), pipeline_mode=pl.Buffered(1))],
        out_specs=pl.BlockSpec((1, tm, 3 * d), lambda b, i: (b, i, 0)),
        out_shape=jax.ShapeDtypeStruct((bsz, s, 3 * d), BF16),
        compiler_params=_params("parallel", "parallel"),
        name="qkv_proj",
    )(x, mod, mod, w_qkv)


def _bias_kernel(rpb_ref, o_ref):
    h = pl.program_id(0)
    n_dr = 2 * NA_WIN_ROWS - 1
    n_dc = 2 * NA_WIN_COLS - 1
    q = lax.broadcasted_iota(jnp.int32, (GRID_W, GRID_W), 0)
    k = lax.broadcasted_iota(jnp.int32, (GRID_W, GRID_W), 1)
    col_start = jnp.clip(q - NA_WIN_COLS // 2, 0, GRID_W - NA_WIN_COLS)
    col_in = (k >= col_start) & (k < col_start + NA_WIN_COLS)
    dc = jnp.clip(k - q + NA_WIN_COLS - 1, 0, n_dc - 1)
    tiles = []
    for dr in range(n_dr):
        t = jnp.full((GRID_W, GRID_W), MASK_VALUE, F32)
        for d in range(n_dc):
            val = rpb_ref[(h * n_dr + dr) * n_dc + d]
            t = jnp.where(col_in & (dc == d), val, t)
        tiles.append(t)
    for o in range(NA_WIN_ROWS):
        o_ref[o, 0] = jnp.concatenate(tiles[o:o + NA_WIN_ROWS], axis=1)


def _bias_table(rpb):
    bands = pl.pallas_call(
        _bias_kernel,
        grid=(NA_HEADS,),
        in_specs=[pl.BlockSpec(memory_space=pltpu.SMEM)],
        out_specs=pl.BlockSpec((NA_WIN_ROWS, 1, GRID_W, BAND), lambda h: (0, h, 0, 0)),
        out_shape=jax.ShapeDtypeStruct((NA_WIN_ROWS, NA_HEADS, GRID_W, BAND), F32),
        compiler_params=_params("parallel"),
        name="rpb_table",
    )(rpb.reshape(-1))
    return bands.reshape(NA_WIN_ROWS, NA_HEADS * GRID_W, BAND)


def _attn_kernel(q_ref, k_ref, v_ref, bias_ref, o_ref, *, rows):
    lane_head = lax.broadcasted_iota(jnp.int32, (GRID_W, MXU_DIM), 1) // NA_HEAD_DIM

    def one_row(r, carry):
        rs = jnp.clip(r - NA_WIN_ROWS // 2, 0, rows - NA_WIN_ROWS)
        off = rs - r + NA_WIN_ROWS - 1
        qr = q_ref[0, pl.ds(pl.multiple_of(r * GRID_W, GRID_W), GRID_W), :]
        zero = jnp.zeros_like(qr)
        q_heads = jnp.concatenate([jnp.where(lane_head == h, qr, zero) for h in range(HEAD_GROUP)], axis=0)
        band = pl.ds(pl.multiple_of(rs * GRID_W, GRID_W), BAND)
        kb = k_ref[0, band, :]
        s = lax.dot_general(q_heads, kb, (((1,), (1,)), ((), ())), preferred_element_type=F32)
        s = s + bias_ref[off]
        m = jnp.max(s, axis=-1, keepdims=True)
        p = jnp.exp(s - m)
        l = jnp.sum(p, axis=-1, keepdims=True)
        vb = v_ref[0, band, :]
        g = jnp.dot(p.astype(BF16), vb, preferred_element_type=F32) / l
        out = jnp.zeros((GRID_W, MXU_DIM), F32)
        for h in range(HEAD_GROUP):
            out = jnp.where(lane_head == h, g[h * GRID_W:(h + 1) * GRID_W, :], out)
        o_ref[0, pl.ds(pl.multiple_of(r * GRID_W, GRID_W), GRID_W), :] = out.astype(BF16)
        return carry

    lax.fori_loop(0, rows, one_row, 0, unroll=ATTN_UNROLL)


def _attention(qkv, bias):
    bsz, s, _ = qkv.shape
    rows = s // GRID_W
    assert rows >= NA_WIN_ROWS
    blk = lambda part: pl.BlockSpec((1, s, MXU_DIM), lambda b, g: (b, 0, part * N_HEAD_GROUPS + g))
    return pl.pallas_call(
        functools.partial(_attn_kernel, rows=rows),
        grid=(bsz, N_HEAD_GROUPS),
        in_specs=[blk(0), blk(1), blk(2),
                  pl.BlockSpec((NA_WIN_ROWS, HEAD_GROUP * GRID_W, BAND), lambda b, g: (0, g, 0))],
        out_specs=pl.BlockSpec((1, s, MXU_DIM), lambda b, g: (b, 0, g)),
        out_shape=jax.ShapeDtypeStruct((bsz, s, D_MODEL), BF16),
        compiler_params=_params("parallel", "parallel"),
        name="nbr_attention",
    )(qkv, qkv, qkv, bias)


def _proj_ln_kernel(y_ref, w_ref, x_ref, gate_ref, g_ref, b_ref, o_ref):
    for j in range(y_ref.shape[1] // EPILOGUE_ROWS):
        rows = slice(j * EPILOGUE_ROWS, (j + 1) * EPILOGUE_ROWS)
        y = jnp.dot(y_ref[0, rows, :], w_ref[...], preferred_element_type=F32)
        h = ALPHA * x_ref[0, rows, :] + (1.0 + gate_ref[0]) * y
        o_ref[0, rows, :] = _layer_norm(h, g_ref[...], b_ref[...])


def _proj_ln(y, w, x, mod, ln_g, ln_b, tm=1024):
    bsz, s, d = x.shape
    vec = pl.BlockSpec((1, d), lambda b, i: (0, 0))
    return pl.pallas_call(
        _proj_ln_kernel,
        grid=(bsz, s // tm),
        in_specs=[pl.BlockSpec((1, tm, d), lambda b, i: (b, i, 0)),
                  pl.BlockSpec((d, d), lambda b, i: (0, 0)),
                  pl.BlockSpec((1, tm, d), lambda b, i: (b, i, 0)),
                  _mod_spec(2), vec, vec],
        out_specs=pl.BlockSpec((1, tm, d), lambda b, i: (b, i, 0)),
        out_shape=jax.ShapeDtypeStruct((bsz, s, d), F32),
        compiler_params=_params("parallel", "parallel"),
        name="proj_ln",
    )(y, w, x, mod, ln_g.reshape(1, d), ln_b.reshape(1, d))


def _gelu_tanh(x):
    c = np.float32(np.sqrt(2.0 / np.pi))
    return x * (0.5 * (1.0 + jnp.tanh(c * (x + 0.044715 * (x * x * x)))))


def _ffn_kernel(x_ref, xp_ref, xn_ref, sh_ref, sc_ref, gate_ref, wup_ref, cw_ref, cb_ref, wd_ref,
                g_ref, b_ref, o_ref, u_ref, *, tm):
    i = pl.program_id(1)
    scale = 1.0 + sc_ref[0]
    shift = sh_ref[0]
    x = x_ref[0]
    prev = jnp.where(i > 0, xp_ref[0] * scale + shift, 0.0)
    nxt = jnp.where(i < pl.num_programs(1) - 1, xn_ref[0] * scale + shift, 0.0)
    u_ref[0:HALO, :] = prev.astype(BF16)
    u_ref[HALO:HALO + tm, :] = (x * scale + shift).astype(BF16)
    u_ref[HALO + tm:, :] = nxt.astype(BF16)
    ext = tm + 2 * HALO
    cols = lambda c: slice(c * FF_CHUNK, (c + 1) * FF_CHUNK)

    def up_proj(c):
        a = jnp.dot(u_ref[...], wup_ref[:, cols(c)], preferred_element_type=F32)
        gt = jnp.dot(u_ref[HALO:HALO + tm, :], wup_ref[:, cols(N_FF_CHUNKS + c)], preferred_element_type=F32)
        return a, gt

    a, gt = up_proj(0)
    hidden = []
    for c in range(N_FF_CHUNKS):
        ahead = up_proj(c + 1) if c + 1 < N_FF_CHUNKS else None
        cw = cw_ref[:, cols(c)]
        a_prev = pltpu.roll(a, 1, 0)[HALO:HALO + tm]
        a_next = pltpu.roll(a, ext - 1, 0)[HALO:HALO + tm]
        conv = cb_ref[:, cols(c)] + a_prev * cw[0:1] + a[HALO:HALO + tm] * cw[1:2] + a_next * cw[2:3]
        hidden.append((_gelu_tanh(conv) * gt).astype(BF16))
        if ahead is not None:
            a, gt = ahead

    hid = jnp.concatenate(hidden, axis=1)
    for j in range(tm // EPILOGUE_ROWS):
        rows = slice(j * EPILOGUE_ROWS, (j + 1) * EPILOGUE_ROWS)
        y = jnp.dot(hid[rows], wd_ref[...], preferred_element_type=F32)
        h = ALPHA * x[rows] + (1.0 + gate_ref[0]) * y
        o_ref[0, rows, :] = _layer_norm(h, g_ref[...], b_ref[...])


def _conv_ffn(x, mod, w_up, conv_w, conv_b, w_down, ln_g, ln_b, tm=512):
    bsz, s, d = x.shape
    nblk = s // tm
    hb = tm // HALO
    n_halo = s // HALO
    vec = pl.BlockSpec((1, d), lambda b, i: (0, 0))
    full = lambda a: pl.BlockSpec(a.shape, lambda b, i: (0, 0), pipeline_mode=pl.Buffered(1))
    wup = w_up.astype(BF16)
    wd = w_down.astype(BF16)
    cb = conv_b.reshape(1, D_FF)
    return pl.pallas_call(
        functools.partial(_ffn_kernel, tm=tm),
        grid=(bsz, nblk),
        in_specs=[pl.BlockSpec((1, tm, d), lambda b, i: (b, i, 0)),
                  pl.BlockSpec((1, HALO, d), lambda b, i: (b, jnp.maximum(i * hb - 1, 0), 0)),
                  pl.BlockSpec((1, HALO, d), lambda b, i: (b, jnp.minimum((i + 1) * hb, n_halo - 1), 0)),
                  _mod_spec(3), _mod_spec(4), _mod_spec(5),
                  full(wup), full(conv_w), full(cb), full(wd), vec, vec],
        out_specs=pl.BlockSpec((1, tm, d), lambda b, i: (b, i, 0)),
        out_shape=jax.ShapeDtypeStruct((bsz, s, d), F32),
        scratch_shapes=[pltpu.VMEM((tm + 2 * HALO, d), BF16)],
        compiler_params=_params("parallel", "arbitrary"),
        name="conv_ffn",
    )(x, x, x, mod, mod, mod, wup, conv_w, cb, wd, ln_g.reshape(1, d), ln_b.reshape(1, d))


FFT_N1 = 64
FFT_N2 = 64
FFT_ROWS = 512
FFT_UNROLL = 32
FFT_PITCH = 72


def _fourier_consts():
    s = FFT_N1 * FFT_N2
    c = np.arange(FN_GROUP_DIM)
    ang = 2.0 * np.pi * np.outer(c, c) / FN_GROUP_DIM
    blk = lambda m: np.kron(np.eye(MXU_DIM // FN_GROUP_DIM), m)
    scale = 1.0 / np.sqrt(s * FN_GROUP_DIM)
    chan = np.concatenate([blk(np.cos(ang)), -blk(np.sin(ang))], axis=1) * scale
    k1 = np.arange(FFT_N1)[None, :, None]
    a = np.arange(FFT_N1)[None, None, :]
    b = np.arange(FFT_N2)[:, None, None]
    th = 2.0 * np.pi * ((k1 * (FFT_N2 * a + b)) % s) / s
    st1 = np.concatenate([np.concatenate([np.cos(th), np.sin(th)], axis=2),
                          np.concatenate([-np.sin(th), np.cos(th)], axis=2)], axis=1)
    k2 = np.arange(FFT_N2)[:, None]
    bb = np.arange(FFT_N2)[None, :]
    ph = 2.0 * np.pi * ((k2 * bb) % FFT_N2) / FFT_N2
    st2 = np.concatenate([np.cos(ph), np.sin(ph)], axis=1)
    f32 = lambda m: jnp.asarray(m.astype(np.float32))
    return f32(chan).astype(BF16), f32(st1).astype(BF16), f32(st2).astype(BF16)


def _fourier_kernel(x_ref, sh_ref, sc_ref, chan_ref, st1_ref, st2_ref, o_ref, z_ref, y_ref):
    s = FFT_N1 * FFT_N2
    n_parts = 2 * MXU_DIM // LANES
    groups = FFT_ROWS // FFT_N2
    scale = 1.0 + sc_ref[0]
    shift = sh_ref[0]
    group_rows = lambda g: pl.ds(pl.multiple_of(g * FFT_PITCH, 8), FFT_N2)

    def chan_dft(j, carry):
        rows = pl.ds(pl.multiple_of(j * FFT_ROWS, FFT_ROWS), FFT_ROWS)
        u = (x_ref[0, rows, :] * scale + shift).astype(BF16)
        z = jnp.dot(u, chan_ref[...], preferred_element_type=F32)
        for i in range(groups):
            for p in range(n_parts):
                z_ref[p, group_rows(j * groups + i), :] = z[i * FFT_N2:(i + 1) * FFT_N2, p * LANES:(p + 1) * LANES]
        return carry

    lax.fori_loop(0, s // FFT_ROWS, chan_dft, 0, unroll=4)

    def gather(ref, r):
        take = pl.ds(r, FFT_N1, stride=FFT_PITCH)
        re = jnp.concatenate([ref[0, take, :], ref[1, take, :]], axis=1)
        im = jnp.concatenate([ref[2, take, :], ref[3, take, :]], axis=1)
        return jnp.concatenate([re, im], axis=0).astype(BF16)

    def stage1(b, carry):
        y = jnp.dot(st1_ref[b], gather(z_ref, b), preferred_element_type=F32)
        for p in range(n_parts):
            half, lane = divmod(p, 2)
            y_ref[p, group_rows(b), :] = y[half * FFT_N1:(half + 1) * FFT_N1, lane * LANES:(lane + 1) * LANES]
        return carry

    lax.fori_loop(0, FFT_N2, stage1, 0, unroll=FFT_UNROLL)

    def stage2(k1, carry):
        out = jnp.dot(st2_ref[...], gather(y_ref, k1), preferred_element_type=F32)
        for lane in range(MXU_DIM // LANES):
            z_ref[lane, pl.ds(k1, FFT_N2, stride=FFT_PITCH), :] = out[:, lane * LANES:(lane + 1) * LANES]
        return carry

    lax.fori_loop(0, FFT_N1, stage2, 0, unroll=FFT_UNROLL)
    for k2 in range(FFT_N2):
        src = slice(k2 * FFT_PITCH, k2 * FFT_PITCH + FFT_N1)
        o_ref[0, k2 * FFT_N1:(k2 + 1) * FFT_N1, :] = jnp.concatenate(
            [z_ref[0, src, :], z_ref[1, src, :]], axis=1).astype(BF16)


def _fourier_mix(x, mod):
    bsz, s, d = x.shape
    assert s == FFT_N1 * FFT_N2
    chan, st1, st2 = _fourier_consts()
    mspec = lambda piece: pl.BlockSpec((1, 1, MXU_DIM), lambda b, g: (b, 0, piece * (d // MXU_DIM) + g))
    buf = pltpu.VMEM((2 * MXU_DIM // LANES, FFT_N1 * FFT_PITCH, LANES), F32)
    return pl.pallas_call(
        _fourier_kernel,
        grid=(bsz, d // MXU_DIM),
        in_specs=[pl.BlockSpec((1, s, MXU_DIM), lambda b, g: (b, 0, g)),
                  mspec(0), mspec(1),
                  pl.BlockSpec(chan.shape, lambda b, g: (0, 0)),
                  pl.BlockSpec(st1.shape, lambda b, g: (0, 0, 0)),
                  pl.BlockSpec(st2.shape, lambda b, g: (0, 0))],
        out_specs=pl.BlockSpec((1, s, MXU_DIM), lambda b, g: (b, 0, g)),
        out_shape=jax.ShapeDtypeStruct((bsz, s, d), BF16),
        scratch_shapes=[buf, buf],
        compiler_params=_params("parallel", "parallel"),
        name="fourier_mix",
    )(x, mod, mod, chan, st1, st2)


def kernel(x, c, ada_w, ada_b, na_w_qkv, na_rpb, na_w_o, fn_w_o, ln1_g, ln1_b,
           ffn_w_up, ffn_conv_w, ffn_conv_b, ffn_w_down, ln2_g, ln2_b):
    bsz = x.shape[0]
    mod_all = _modulation(c, ada_w, ada_b)
    for i in range(DEPTH):
        mod = mod_all[i].reshape(bsz, 1, 6 * D_MODEL)
        j = i // 2
        if i % 2 == 0:
            qkv = _qkv_proj(x, mod, na_w_qkv[j].astype(BF16))
            y = _attention(qkv, _bias_table(na_rpb[j]))
            w_o = na_w_o[j]
        else:
            y = _fourier_mix(x, mod)
            w_o = fn_w_o[j]
        x = _proj_ln(y, w_o.astype(BF16), x, mod, ln1_g[i], ln1_b[i])
        x = _conv_ffn(x, mod, ffn_w_up[i], ffn_conv_w[i], ffn_conv_b[i], ffn_w_down[i], ln2_g[i], ln2_b[i])
    return x
```

```python
import functools

import numpy as np
import jax
import jax.numpy as jnp
from jax import lax
from jax.experimental import pallas as pl
from jax.experimental.pallas import tpu as pltpu

F32 = jnp.float32
BF16 = jnp.bfloat16

D_MODEL = 1024
DEPTH = 2
GRID_W = 64
NA_HEADS = 16
NA_HEAD_DIM = D_MODEL // NA_HEADS
NA_WIN_ROWS = 8
NA_WIN_COLS = 16
FN_GROUP_DIM = 128
D_FF = 2816
CONV_W = 3
LN_EPS = 1e-5
ALPHA = (2.0 * DEPTH) ** 0.25

LANES = 128
MXU_DIM = 256
VMEM_LIMIT = 56 * 1024 * 1024

HEAD_GROUP = MXU_DIM // NA_HEAD_DIM
N_HEAD_GROUPS = NA_HEADS // HEAD_GROUP
BAND = NA_WIN_ROWS * GRID_W
FF_CHUNK = MXU_DIM
N_FF_CHUNKS = D_FF // FF_CHUNK
HALO = 16
EPILOGUE_ROWS = 256
MASK_VALUE = -1e30
ATTN_UNROLL = 32


def _params(*sem):
    return pltpu.CompilerParams(dimension_semantics=sem, vmem_limit_bytes=VMEM_LIMIT)


def _layer_norm(h, g, b):
    mu = jnp.mean(h, axis=-1, keepdims=True)
    d = h - mu
    var = jnp.mean(d * d, axis=-1, keepdims=True)
    return d * lax.rsqrt(var + LN_EPS) * g + b


def _mod_kernel(c_ref, w_ref, b_ref, o_ref):
    c = c_ref[...]
    cs = c * jax.nn.sigmoid(c)
    o_ref[0] = jnp.dot(cs, w_ref[0], preferred_element_type=F32,
                       precision=lax.Precision.HIGHEST) + b_ref[0]


def _modulation(c, ada_w, ada_b):
    depth, d, n = ada_w.shape
    bsz = c.shape[0]
    tn = 1536
    return pl.pallas_call(
        _mod_kernel,
        grid=(depth, n // tn),
        in_specs=[pl.BlockSpec((bsz, d), lambda i, j: (0, 0)),
                  pl.BlockSpec((1, d, tn), lambda i, j: (i, 0, j)),
                  pl.BlockSpec((1, 1, tn), lambda i, j: (i, 0, j))],
        out_specs=pl.BlockSpec((1, bsz, tn), lambda i, j: (i, 0, j)),
        out_shape=jax.ShapeDtypeStruct((depth, bsz, n), F32),
        compiler_params=_params("parallel", "parallel"),
        name="adaln_mod",
    )(c, ada_w, ada_b.reshape(depth, 1, n))


def _mod_spec(piece):
    return pl.BlockSpec((1, 1, D_MODEL), lambda b, i: (b, 0, piece))


def _qkv_kernel(x_ref, sh_ref, sc_ref, w_ref, o_ref):
    u = (x_ref[0] * (1.0 + sc_ref[0]) + sh_ref[0]).astype(BF16)
    for j in range(3):
        cols = slice(j * D_MODEL, (j + 1) * D_MODEL)
        acc = jnp.dot(u, w_ref[:, cols], preferred_element_type=F32)
        if j == 0:
            acc = acc * (NA_HEAD_DIM ** -0.5)
        o_ref[0, :, cols] = acc.astype(BF16)


def _qkv_proj(x, mod, w_qkv, tm=1024):
    bsz, s, d = x.shape
    return pl.pallas_call(
        _qkv_kernel,
        grid=(bsz, s // tm),
        in_specs=[pl.BlockSpec((1, tm, d), lambda b, i: (b, i, 0)),
                  _mod_spec(0), _mod_spec(1),
                  pl.BlockSpec((d, 3 * d), lambda b, i: (0, 0), pipeline_mode=pl.Buffered(1))],
        out_specs=pl.BlockSpec((1, tm, 3 * d), lambda b, i: (b, i, 0)),
        out_shape=jax.ShapeDtypeStruct((bsz, s, 3 * d), BF16),
        compiler_params=_params("parallel", "parallel"),
        name="qkv_proj",
    )(x, mod, mod, w_qkv)


def _bias_kernel(rpb_ref, o_ref):
    h = pl.program_id(0)
    n_dr = 2 * NA_WIN_ROWS - 1
    n_dc = 2 * NA_WIN_COLS - 1
    q = lax.broadcasted_iota(jnp.int32, (GRID_W, GRID_W), 0)
    k = lax.broadcasted_iota(jnp.int32, (GRID_W, GRID_W), 1)
    col_start = jnp.clip(q - NA_WIN_COLS // 2, 0, GRID_W - NA_WIN_COLS)
    col_in = (k >= col_start) & (k < col_start + NA_WIN_COLS)
    dc = jnp.clip(k - q + NA_WIN_COLS - 1, 0, n_dc - 1)
    tiles = []
    for dr in range(n_dr):
        t = jnp.full((GRID_W, GRID_W), MASK_VALUE, F32)
        for d in range(n_dc):
            val = rpb_ref[(h * n_dr + dr) * n_dc + d]
            t = jnp.where(col_in & (dc == d), val, t)
        tiles.append(t)
    for o in range(NA_WIN_ROWS):
        o_ref[o, 0] = jnp.concatenate(tiles[o:o + NA_WIN_ROWS], axis=1)


def _bias_table(rpb):
    bands = pl.pallas_call(
        _bias_kernel,
        grid=(NA_HEADS,),
        in_specs=[pl.BlockSpec(memory_space=pltpu.SMEM)],
        out_specs=pl.BlockSpec((NA_WIN_ROWS, 1, GRID_W, BAND), lambda h: (0, h, 0, 0)),
        out_shape=jax.ShapeDtypeStruct((NA_WIN_ROWS, NA_HEADS, GRID_W, BAND), F32),
        compiler_params=_params("parallel"),
        name="rpb_table",
    )(rpb.reshape(-1))
    return bands.reshape(NA_WIN_ROWS, NA_HEADS * GRID_W, BAND)


def _attn_kernel(q_ref, k_ref, v_ref, bias_ref, o_ref, *, rows):
    lane_head = lax.broadcasted_iota(jnp.int32, (GRID_W, MXU_DIM), 1) // NA_HEAD_DIM

    def one_row(r, carry):
        rs = jnp.clip(r - NA_WIN_ROWS // 2, 0, rows - NA_WIN_ROWS)
        off = rs - r + NA_WIN_ROWS - 1
        qr = q_ref[0, pl.ds(pl.multiple_of(r * GRID_W, GRID_W), GRID_W), :]
        zero = jnp.zeros_like(qr)
        q_heads = jnp.concatenate([jnp.where(lane_head == h, qr, zero) for h in range(HEAD_GROUP)], axis=0)
        band = pl.ds(pl.multiple_of(rs * GRID_W, GRID_W), BAND)
        kb = k_ref[0, band, :]
        s = lax.dot_general(q_heads, kb, (((1,), (1,)), ((), ())), preferred_element_type=F32)
        s = s + bias_ref[off]
        m = jnp.max(s, axis=-1, keepdims=True)
        p = jnp.exp(s - m)
        l = jnp.sum(p, axis=-1, keepdims=True)
        vb = v_ref[0, band, :]
        g = jnp.dot(p.astype(BF16), vb, preferred_element_type=F32) / l
        out = jnp.zeros((GRID_W, MXU_DIM), F32)
        for h in range(HEAD_GROUP):
            out = jnp.where(lane_head == h, g[h * GRID_W:(h + 1) * GRID_W, :], out)
        o_ref[0, pl.ds(pl.multiple_of(r * GRID_W, GRID_W), GRID_W), :] = out.astype(BF16)
        return carry

    lax.fori_loop(0, rows, one_row, 0, unroll=ATTN_UNROLL)


def _attention(qkv, bias):
    bsz, s, _ = qkv.shape
    rows = s // GRID_W
    assert rows >= NA_WIN_ROWS
    blk = lambda part: pl.BlockSpec((1, s, MXU_DIM), lambda b, g: (b, 0, part * N_HEAD_GROUPS + g))
    return pl.pallas_call(
        functools.partial(_attn_kernel, rows=rows),
        grid=(bsz, N_HEAD_GROUPS),
        in_specs=[blk(0), blk(1), blk(2),
                  pl.BlockSpec((NA_WIN_ROWS, HEAD_GROUP * GRID_W, BAND), lambda b, g: (0, g, 0))],
        out_specs=pl.BlockSpec((1, s, MXU_DIM), lambda b, g: (b, 0, g)),
        out_shape=jax.ShapeDtypeStruct((bsz, s, D_MODEL), BF16),
        compiler_params=_params("parallel", "parallel"),
        name="nbr_attention",
    )(qkv, qkv, qkv, bias)


def _proj_ln_kernel(y_ref, w_ref, x_ref, gate_ref, g_ref, b_ref, o_ref):
    for j in range(y_ref.shape[1] // EPILOGUE_ROWS):
        rows = slice(j * EPILOGUE_ROWS, (j + 1) * EPILOGUE_ROWS)
        y = jnp.dot(y_ref[0, rows, :], w_ref[...], preferred_element_type=F32)
        h = ALPHA * x_ref[0, rows, :] + (1.0 + gate_ref[0]) * y
        o_ref[0, rows, :] = _layer_norm(h, g_ref[...], b_ref[...])


def _proj_ln(y, w, x, mod, ln_g, ln_b, tm=1024):
    bsz, s, d = x.shape
    vec = pl.BlockSpec((1, d), lambda b, i: (0, 0))
    return pl.pallas_call(
        _proj_ln_kernel,
        grid=(bsz, s // tm),
        in_specs=[pl.BlockSpec((1, tm, d), lambda b, i: (b, i, 0)),
                  pl.BlockSpec((d, d), lambda b, i: (0, 0)),
                  pl.BlockSpec((1, tm, d), lambda b, i: (b, i, 0)),
                  _mod_spec(2), vec, vec],
        out_specs=pl.BlockSpec((1, tm, d), lambda b, i: (b, i, 0)),
        out_shape=jax.ShapeDtypeStruct((bsz, s, d), F32),
        compiler_params=_params("parallel", "parallel"),
        name="proj_ln",
    )(y, w, x, mod, ln_g.reshape(1, d), ln_b.reshape(1, d))


def _gelu_tanh(x):
    c = float(np.sqrt(2.0 / np.pi))
    return x * (0.5 * (1.0 + jnp.tanh(c * (x + 0.044715 * (x * x * x)))))


def _ffn_kernel(x_ref, xp_ref, xn_ref, sh_ref, sc_ref, gate_ref, wup_ref, cw_ref, cb_ref, wd_ref,
                g_ref, b_ref, o_ref, u_ref, *, tm):
    i = pl.program_id(1)
    scale = 1.0 + sc_ref[0]
    shift = sh_ref[0]
    x = x_ref[0]
    prev = jnp.where(i > 0, xp_ref[0] * scale + shift, 0.0)
    nxt = jnp.where(i < pl.num_programs(1) - 1, xn_ref[0] * scale + shift, 0.0)
    u_ref[0:HALO, :] = prev.astype(BF16)
    u_ref[HALO:HALO + tm, :] = (x * scale + shift).astype(BF16)
    u_ref[HALO + tm:, :] = nxt.astype(BF16)
    ext = tm + 2 * HALO
    cols = lambda c: slice(c * FF_CHUNK, (c + 1) * FF_CHUNK)

    def up_proj(c):
        a = jnp.dot(u_ref[...], wup_ref[:, cols(c)], preferred_element_type=F32)
        gt = jnp.dot(u_ref[HALO:HALO + tm, :], wup_ref[:, cols(N_FF_CHUNKS + c)], preferred_element_type=F32)
        return a, gt

    a, gt = up_proj(0)
    hidden = []
    for c in range(N_FF_CHUNKS):
        ahead = up_proj(c + 1) if c + 1 < N_FF_CHUNKS else None
        cw = cw_ref[:, cols(c)]
        a_prev = pltpu.roll(a, 1, 0)[HALO:HALO + tm]
        a_next = pltpu.roll(a, ext - 1, 0)[HALO:HALO + tm]
        conv = cb_ref[:, cols(c)] + a_prev * cw[0:1] + a[HALO:HALO + tm] * cw[1:2] + a_next * cw[2:3]
        hidden.append(_gelu_tanh(conv.astype(BF16)) * gt.astype(BF16))
        if ahead is not None:
            a, gt = ahead

    hid = jnp.concatenate(hidden, axis=1)
    for j in range(tm // EPILOGUE_ROWS):
        rows = slice(j * EPILOGUE_ROWS, (j + 1) * EPILOGUE_ROWS)
        y = jnp.dot(hid[rows], wd_ref[...], preferred_element_type=F32)
        h = ALPHA * x[rows] + (1.0 + gate_ref[0]) * y
        o_ref[0, rows, :] = _layer_norm(h, g_ref[...], b_ref[...])


def _conv_ffn(x, mod, w_up, conv_w, conv_b, w_down, ln_g, ln_b, tm=512):
    bsz, s, d = x.shape
    nblk = s // tm
    hb = tm // HALO
    n_halo = s // HALO
    vec = pl.BlockSpec((1, d), lambda b, i: (0, 0))
    full = lambda a: pl.BlockSpec(a.shape, lambda b, i: (0, 0), pipeline_mode=pl.Buffered(1))
    wup = w_up.astype(BF16)
    wd = w_down.astype(BF16)
    cb = conv_b.reshape(1, D_FF)
    return pl.pallas_call(
        functools.partial(_ffn_kernel, tm=tm),
        grid=(bsz, nblk),
        in_specs=[pl.BlockSpec((1, tm, d), lambda b, i: (b, i, 0)),
                  pl.BlockSpec((1, HALO, d), lambda b, i: (b, jnp.maximum(i * hb - 1, 0), 0)),
                  pl.BlockSpec((1, HALO, d), lambda b, i: (b, jnp.minimum((i + 1) * hb, n_halo - 1), 0)),
                  _mod_spec(3), _mod_spec(4), _mod_spec(5),
                  full(wup), full(conv_w), full(cb), full(wd), vec, vec],
        out_specs=pl.BlockSpec((1, tm, d), lambda b, i: (b, i, 0)),
        out_shape=jax.ShapeDtypeStruct((bsz, s, d), F32),
        scratch_shapes=[pltpu.VMEM((tm + 2 * HALO, d), BF16)],
        compiler_params=_params("parallel", "arbitrary"),
        name="conv_ffn",
    )(x, x, x, mod, mod, mod, wup, conv_w, cb, wd, ln_g.reshape(1, d), ln_b.reshape(1, d))


FFT_N1 = 64
FFT_N2 = 64
FFT_ROWS = 512
FFT_UNROLL = 32
FFT_PITCH = 72


def _fourier_consts():
    s = FFT_N1 * FFT_N2
    c = np.arange(FN_GROUP_DIM)
    ang = 2.0 * np.pi * np.outer(c, c) / FN_GROUP_DIM
    blk = lambda m: np.kron(np.eye(MXU_DIM // FN_GROUP_DIM), m)
    scale = 1.0 / np.sqrt(s * FN_GROUP_DIM)
    chan = np.concatenate([blk(np.cos(ang)), -blk(np.sin(ang))], axis=1) * scale
    k1 = np.arange(FFT_N1)[None, :, None]
    a = np.arange(FFT_N1)[None, None, :]
    b = np.arange(FFT_N2)[:, None, None]
    th = 2.0 * np.pi * ((k1 * (FFT_N2 * a + b)) % s) / s
    st1 = np.concatenate([np.concatenate([np.cos(th), np.sin(th)], axis=2),
                          np.concatenate([-np.sin(th), np.cos(th)], axis=2)], axis=1)
    k2 = np.arange(FFT_N2)[:, None]
    bb = np.arange(FFT_N2)[None, :]
    ph = 2.0 * np.pi * ((k2 * bb) % FFT_N2) / FFT_N2
    st2 = np.concatenate([np.cos(ph), np.sin(ph)], axis=1)
    f32 = lambda m: jnp.asarray(m.astype(np.float32))
    return f32(chan).astype(BF16), f32(st1).astype(BF16), f32(st2).astype(BF16)


def _fourier_kernel(x_ref, sh_ref, sc_ref, chan_ref, st1_ref, st2_ref, o_ref, z_ref, y_ref):
    s = FFT_N1 * FFT_N2
    n_parts = 2 * MXU_DIM // LANES
    groups = FFT_ROWS // FFT_N2
    scale = 1.0 + sc_ref[0]
    shift = sh_ref[0]
    group_rows = lambda g: pl.ds(pl.multiple_of(g * FFT_PITCH, 8), FFT_N2)

    def chan_dft(j, carry):
        rows = pl.ds(pl.multiple_of(j * FFT_ROWS, FFT_ROWS), FFT_ROWS)
        u = (x_ref[0, rows, :] * scale + shift).astype(BF16)
        z = jnp.dot(u, chan_ref[...], preferred_element_type=F32)
        for i in range(groups):
            for p in range(n_parts):
                z_ref[p, group_rows(j * groups + i), :] = z[i * FFT_N2:(i + 1) * FFT_N2, p * LANES:(p + 1) * LANES]
        return carry

    lax.fori_loop(0, s // FFT_ROWS, chan_dft, 0, unroll=4)

    def gather(ref, r):
        take = pl.ds(r, FFT_N1, stride=FFT_PITCH)
        re = jnp.concatenate([ref[0, take, :], ref[1, take, :]], axis=1)
        im = jnp.concatenate([ref[2, take, :], ref[3, take, :]], axis=1)
        return jnp.concatenate([re, im], axis=0).astype(BF16)

    def stage1(b, carry):
        y = jnp.dot(st1_ref[b], gather(z_ref, b), preferred_element_type=F32)
        for p in range(n_parts):
            half, lane = divmod(p, 2)
            y_ref[p, group_rows(b), :] = y[half * FFT_N1:(half + 1) * FFT_N1, lane * LANES:(lane + 1) * LANES]
        return carry

    lax.fori_loop(0, FFT_N2, stage1, 0, unroll=FFT_UNROLL)

    def stage2(k1, carry):
        out = jnp.dot(st2_ref[...], gather(y_ref, k1), preferred_element_type=F32)
        for lane in range(MXU_DIM // LANES):
            z_ref[lane, pl.ds(k1, FFT_N2, stride=FFT_PITCH), :] = out[:, lane * LANES:(lane + 1) * LANES]
        return carry

    lax.fori_loop(0, FFT_N1, stage2, 0, unroll=FFT_UNROLL)
    for k2 in range(FFT_N2):
        src = slice(k2 * FFT_PITCH, k2 * FFT_PITCH + FFT_N1)
        o_ref[0, k2 * FFT_N1:(k2 + 1) * FFT_N1, :] = jnp.concatenate(
            [z_ref[0, src, :], z_ref[1, src, :]], axis=1).astype(BF16)


def _fourier_mix(x, mod):
    bsz, s, d = x.shape
    assert s == FFT_N1 * FFT_N2
    chan, st1, st2 = _fourier_consts()
    mspec = lambda piece: pl.BlockSpec((1, 1, MXU_DIM), lambda b, g: (b, 0, piece * (d // MXU_DIM) + g))
    buf = pltpu.VMEM((2 * MXU_DIM // LANES, FFT_N1 * FFT_PITCH, LANES), F32)
    return pl.pallas_call(
        _fourier_kernel,
        grid=(bsz, d // MXU_DIM),
        in_specs=[pl.BlockSpec((1, s, MXU_DIM), lambda b, g: (b, 0, g)),
                  mspec(0), mspec(1),
                  pl.BlockSpec(chan.shape, lambda b, g: (0, 0)),
                  pl.BlockSpec(st1.shape, lambda b, g: (0, 0, 0)),
                  pl.BlockSpec(st2.shape, lambda b, g: (0, 0))],
        out_specs=pl.BlockSpec((1, s, MXU_DIM), lambda b, g: (b, 0, g)),
        out_shape=jax.ShapeDtypeStruct((bsz, s, d), BF16),
        scratch_shapes=[buf, buf],
        compiler_params=_params("parallel", "parallel"),
        name="fourier_mix",
    )(x, mod, mod, chan, st1, st2)


def kernel(x, c, ada_w, ada_b, na_w_qkv, na_rpb, na_w_o, fn_w_o, ln1_g, ln1_b,
           ffn_w_up, ffn_conv_w, ffn_conv_b, ffn_w_down, ln2_g, ln2_b):
    bsz = x.shape[0]
    mod_all = _modulation(c, ada_w, ada_b)
    for i in range(DEPTH):
        mod = mod_all[i].reshape(bsz, 1, 6 * D_MODEL)
        j = i // 2
        if i % 2 == 0:
            qkv = _qkv_proj(x, mod, na_w_qkv[j].astype(BF16))
            y = _attention(qkv, _bias_table(na_rpb[j]))
            w_o = na_w_o[j]
        else:
            y = _fourier_mix(x, mod)
            w_o = fn_w_o[j]
        x = _proj_ln(y, w_o.astype(BF16), x, mod, ln1_g[i], ln1_b[i])
        x = _conv_ffn(x, mod, ffn_w_up[i], ffn_conv_w[i], ffn_conv_b[i], ffn_w_down[i], ln2_g[i], ln2_b[i])
    return x
```
